```python
import jax, jax.numpy as jnp
from jax import lax
import numpy as np

D_MODEL = 2048
BATCH = 2
SEQ = 4096
DEPTH = 2
DEC_BATCH = 128
DEC_SEQ = 8
PAST_LEN = 8192
PAGE_SIZE = 128

F32 = jnp.float32
NEG_INF = -1e30
EPS = 1e-6

BRANCH_W = D_MODEL // 2
N_BRANCH = 3
A_HEAD_DIM = 64
A_HEADS = BRANCH_W // A_HEAD_DIM
A_KV_HEADS = 4
A_GROUP = A_HEADS // A_KV_HEADS
A_KV_W = A_KV_HEADS * A_HEAD_DIM
WINDOW = 128
LRU_WIDTH = BRANCH_W
LRU_BLOCKS = 16
LRU_BLOCK = LRU_WIDTH // LRU_BLOCKS
CONV_WIDTH = 4
LRU_C = 8.0
R_HEADS = 4
R_HEAD_DIM = BRANCH_W // R_HEADS
RET_CHUNK = 128
RET_THETA = 10000.0
MEM_LEN = 256
X_HEADS = 4
X_HEAD_DIM = 128
X_W = X_HEADS * X_HEAD_DIM
D_FF = 4 * D_MODEL
SPLIT_SIZES = (BRANCH_W, A_KV_W, A_KV_W, LRU_WIDTH, LRU_WIDTH, BRANCH_W, BRANCH_W, BRANCH_W, BRANCH_W, N_BRANCH * D_MODEL)
IN_W = sum(SPLIT_SIZES)

kernel_name = 'hybrid_swa_rglru_retention_decoder_step'


def rmsnorm(x, g):
    xf = x.astype(F32)
    y = xf * lax.rsqrt(jnp.mean(xf * xf, axis=-1, keepdims=True) + EPS)
    return (y * g.astype(F32)).astype(x.dtype)


def split_in(proj):
    cuts = [int(c) for c in np.cumsum(SPLIT_SIZES)[:-1]]
    return jnp.split(proj, cuts, axis=-1)


def window_mask(qpos, kpos):
    return (kpos <= qpos) & (qpos - kpos < WINDOW) & (kpos >= 0)


def sink_softmax(s, mask, sink):
    s = jnp.where(mask, s, NEG_INF)
    sk = jnp.broadcast_to(sink[:, :, None, None], s.shape[:-1] + (1,))
    return jax.nn.softmax(jnp.concatenate([s, sk], axis=-1), axis=-1)[..., :-1]


def swa_prompt(q, k, v, sink):
    b, t = q.shape[:2]
    nb = t // WINDOW
    qb = q.reshape(b, nb, WINDOW, A_KV_HEADS, A_GROUP, A_HEAD_DIM)
    pad = ((0, 0), (WINDOW, 0), (0, 0), (0, 0))
    kp, vp = jnp.pad(k, pad), jnp.pad(v, pad)

    def band(z):
        prev = z[:, :t].reshape(b, nb, WINDOW, A_KV_HEADS, A_HEAD_DIM)
        cur = z[:, WINDOW:].reshape(b, nb, WINDOW, A_KV_HEADS, A_HEAD_DIM)
        return jnp.concatenate([prev, cur], axis=2)

    kb, vb = band(kp), band(vp)
    blk = jnp.arange(nb)[:, None] * WINDOW
    qpos = blk + jnp.arange(WINDOW)[None, :]
    kpos = blk - WINDOW + jnp.arange(2 * WINDOW)[None, :]
    mask = window_mask(qpos[:, :, None], kpos[:, None, :])
    s = jnp.einsum('bnqhgd,bnkhd->bnhgqk', qb, kb, preferred_element_type=F32) * (A_HEAD_DIM ** -0.5)
    p = sink_softmax(s, mask[None, :, None, None], sink)
    o = jnp.einsum('bnhgqk,bnkhd->bnqhgd', p, vb.astype(F32)).reshape(b, t, BRANCH_W)
    buf = min(WINDOW, PAST_LEN)
    return o, k[:, t - buf:], v[:, t - buf:]


def swa_sample(q, k, v, k_buf, v_buf, sink):
    t = q.shape[1]
    buf = k_buf.shape[1]
    kk = jnp.concatenate([k_buf.astype(k.dtype), k], axis=1)
    vv = jnp.concatenate([v_buf.astype(v.dtype), v], axis=1)
    qpos = PAST_LEN + jnp.arange(t)
    kpos = PAST_LEN - buf + jnp.arange(buf + t)
    mask = window_mask(qpos[:, None], kpos[None, :])
    s = jnp.einsum('bqhgd,bkhd->bhgqk', q, kk, preferred_element_type=F32) * (A_HEAD_DIM ** -0.5)
    p = sink_softmax(s, mask, sink)
    o = jnp.einsum('bhgqk,bkhd->bqhgd', p, vv.astype(F32)).reshape(q.shape[0], t, BRANCH_W)
    return o, kk[:, -buf:], vv[:, -buf:]


def causal_conv(x, buf, w, bias):
    t = x.shape[1]
    xp = jnp.concatenate([buf.astype(x.dtype), x], axis=1)
    y = bias + sum(xp[:, j:j + t] * w[j] for j in range(CONV_WIDTH))
    return y, xp[:, -(CONV_WIDTH - 1):]


def rglru(x, h0, wa, ba, wx, bx, lam, seq_start):
    b, t, _ = x.shape
    x = x.astype(F32)
    xb = x.reshape(b, t, LRU_BLOCKS, LRU_BLOCK)
    r = jax.nn.sigmoid(jnp.einsum('btni,nij->btnj', xb, wa.astype(F32)).reshape(b, t, LRU_WIDTH) + ba.astype(F32))
    i = jax.nn.sigmoid(jnp.einsum('btni,nij->btnj', xb, wx.astype(F32)).reshape(b, t, LRU_WIDTH) + bx.astype(F32))
    log_a = -LRU_C * r * jax.nn.softplus(-lam.astype(F32))
    a = jnp.exp(log_a)
    mult = jnp.sqrt(-jnp.expm1(2.0 * log_a))
    if seq_start:
        mult = mult.at[:, 0].set(1.0)
    u = mult * (i * x)
    u = u.at[:, 0].add(a[:, 0] * h0.astype(F32))
    _, h = lax.associative_scan(lambda e1, e2: (e1[0] * e2[0], e2[0] * e1[1] + e2[1]), (a, u), axis=1)
    return h, h[:, -1]


def rotary_every_two(x, pos):
    half = x.shape[-1] // 2
    inv = 1.0 / (RET_THETA ** jnp.linspace(0.0, 1.0, half, dtype=F32))
    ang = pos.astype(F32)[:, None] * inv[None, :]
    cos = jnp.cos(ang)[None, :, None, :]
    sin = jnp.sin(ang)[None, :, None, :]
    x1, x2 = x[..., 0::2], x[..., 1::2]
    return jnp.stack([x1 * cos - x2 * sin, x2 * cos + x1 * sin], axis=-1).reshape(x.shape)


def ret_log_decay():
    return jnp.log1p(-jnp.exp2(-5.0 - jnp.arange(R_HEADS, dtype=F32)))


def ret_chunk(s, qkv):
    q, k, v = qkv
    c = q.shape[1]
    n = jnp.arange(c, dtype=F32)
    lg = ret_log_decay()
    diff = n[:, None] - n[None, :]
    dmat = jnp.where(diff >= 0, jnp.exp(jnp.maximum(diff, 0.0)[None] * lg[:, None, None]), 0.0)
    inner = jnp.einsum('bqhd,bkhd->bhqk', q, k) * dmat
    q_dec = jnp.exp((n[:, None] + 1.0) * lg[None, :])[None, :, :, None]
    k_dec = jnp.exp((c - 1.0 - n)[:, None] * lg[None, :])[None, :, :, None]
    o = jnp.einsum('bhqk,bkhe->bqhe', inner, v) + jnp.einsum('bqhd,bhde->bqhe', q * q_dec, s)
    s_new = jnp.exp(c * lg)[None, :, None, None] * s + jnp.einsum('bkhd,bkhe->bhde', k * k_dec, v)
    return s_new, o


def retention(q, k, v, s0, pos):
    b, t = q.shape[:2]
    q = rotary_every_two(q.astype(F32), pos)
    k = rotary_every_two(k.astype(F32), pos) * (R_HEAD_DIM ** -0.5)
    v = v.astype(F32)
    c = min(RET_CHUNK, t)
    nc = t // c

    def chunks(z):
        return z.reshape(b, nc, c, R_HEADS, R_HEAD_DIM).swapaxes(0, 1)

    s, o = lax.scan(ret_chunk, s0.astype(F32), (chunks(q), chunks(k), chunks(v)))
    return o.swapaxes(0, 1).reshape(b, t, R_HEADS, R_HEAD_DIM), s


def group_norm(o, g):
    mu = jnp.mean(o, axis=-1, keepdims=True)
    var = jnp.mean(jnp.square(o - mu), axis=-1, keepdims=True)
    return (o - mu) * lax.rsqrt(var + EPS) * g.astype(F32)


def cross_attn(u, mem_k, mem_v, w_xq, w_xo):
    b, t, _ = u.shape
    q = (u @ w_xq).reshape(b, t, X_HEADS, X_HEAD_DIM)
    s = jnp.einsum('bthd,bmhd->bhtm', q, mem_k, preferred_element_type=F32) * (X_HEAD_DIM ** -0.5)
    p = jax.nn.softmax(s, axis=-1)
    o = jnp.einsum('bhtm,bmhd->bthd', p, mem_v.astype(F32)).reshape(b, t, X_W)
    return o.astype(u.dtype) @ w_xo


def trunk_layer(h, pos, mem_k, mem_v, win_k, win_v, conv_buf, lru_h, ret_s, seq_start,
                norm_mix, w_in, attn_sink, conv_w, conv_b, lru_wa, lru_ba, lru_wx, lru_bx, lru_lambda,
                ret_gn, w_branch, w_out, norm_cross, w_xq, w_xo, norm_ffn, w_up, w_down):
    b, t, _ = h.shape
    u = rmsnorm(h, norm_mix)
    qa, ka, va, xr, yr, qc, kc, vc, gc, gates = split_in(u @ w_in)
    qa = qa.reshape(b, t, A_KV_HEADS, A_GROUP, A_HEAD_DIM)
    ka = ka.reshape(b, t, A_KV_HEADS, A_HEAD_DIM)
    va = va.reshape(b, t, A_KV_HEADS, A_HEAD_DIM)
    sink = attn_sink.astype(F32).reshape(A_KV_HEADS, A_GROUP)
    if win_k is None:
        oa, new_wk, new_wv = swa_prompt(qa, ka, va, sink)
    else:
        oa, new_wk, new_wv = swa_sample(qa, ka, va, win_k, win_v, sink)
    xc, new_conv = causal_conv(xr, conv_buf, conv_w, conv_b)
    hr, new_lru = rglru(xc, lru_h, lru_wa, lru_ba, lru_wx, lru_bx, lru_lambda, seq_start)
    ob = jax.nn.gelu(yr.astype(F32)) * hr
    rc, new_ret = retention(qc.reshape(b, t, R_HEADS, R_HEAD_DIM), kc.reshape(b, t, R_HEADS, R_HEAD_DIM),
                            vc.reshape(b, t, R_HEADS, R_HEAD_DIM), ret_s, pos)
    oc = jax.nn.silu(gc.astype(F32)) * group_norm(rc, ret_gn).reshape(b, t, BRANCH_W)
    branches = jnp.stack([oa, ob, oc], axis=2).astype(h.dtype)
    gate = jax.nn.sigmoid(gates.astype(F32).reshape(b, t, N_BRANCH, D_MODEL))
    merged = jnp.einsum('btnd,btnd->btd', gate, jnp.einsum('btnw,nwd->btnd', branches, w_branch).astype(F32))
    h = h + merged.astype(h.dtype) @ w_out
    h = h + cross_attn(rmsnorm(h, norm_cross), mem_k, mem_v, w_xq, w_xo)
    f = jax.nn.relu(rmsnorm(h, norm_ffn) @ w_up)
    h = h + (f * f) @ w_down
    return h, new_wk, new_wv, new_conv, new_lru, new_ret


def setup_inputs(seed: int = 0) -> dict:
    key = jax.random.key(seed)
    ks = iter(jax.random.split(key, 48))

    def nrm(shape, scale):
        return scale * jax.random.normal(next(ks), shape, F32)

    wb = min(WINDOW, PAST_LEN)
    u = jax.random.uniform(next(ks), (DEPTH, LRU_WIDTH), dtype=F32, minval=0.9, maxval=0.999)
    a = u ** (1.0 / LRU_C)
    lam = jnp.log(a) - jnp.log1p(-a)
    return {
        'x_prompt': nrm((BATCH, SEQ, D_MODEL), 1.0),
        'x_sample': nrm((DEC_BATCH, DEC_SEQ, D_MODEL), 1.0),
        'mem_prompt': nrm((BATCH, MEM_LEN, D_MODEL), 1.0),
        'cache_win_k': nrm((DEPTH, DEC_BATCH, wb, A_KV_HEADS, A_HEAD_DIM), 1.0),
        'cache_win_v': nrm((DEPTH, DEC_BATCH, wb, A_KV_HEADS, A_HEAD_DIM), 1.0),
        'state_conv': nrm((DEPTH, DEC_BATCH, CONV_WIDTH - 1, LRU_WIDTH), 1.0),
        'state_lru': nrm((DEPTH, DEC_BATCH, LRU_WIDTH), 0.5),
        'state_ret': nrm((DEPTH, DEC_BATCH, R_HEADS, R_HEAD_DIM, R_HEAD_DIM), 0.3),
        'cache_mem_k': nrm((DEPTH, DEC_BATCH, MEM_LEN, X_HEADS, X_HEAD_DIM), 1.0),
        'cache_mem_v': nrm((DEPTH, DEC_BATCH, MEM_LEN, X_HEADS, X_HEAD_DIM), 1.0),
        'norm_mix': 1.0 + nrm((DEPTH, D_MODEL), 0.01),
        'w_in': nrm((DEPTH, D_MODEL, IN_W), D_MODEL ** -0.5),
        'attn_sink': nrm((DEPTH, A_HEADS), 1.0),
        'conv_w': nrm((DEPTH, CONV_WIDTH, LRU_WIDTH), CONV_WIDTH ** -0.5),
        'conv_b': nrm((DEPTH, LRU_WIDTH), 0.01),
        'lru_wa': nrm((DEPTH, LRU_BLOCKS, LRU_BLOCK, LRU_BLOCK), LRU_BLOCK ** -0.5),
        'lru_ba': nrm((DEPTH, LRU_WIDTH), 0.01),
        'lru_wx': nrm((DEPTH, LRU_BLOCKS, LRU_BLOCK, LRU_BLOCK), LRU_BLOCK ** -0.5),
        'lru_bx': nrm((DEPTH, LRU_WIDTH), 0.01),
        'lru_lambda': lam,
        'ret_gn': 1.0 + nrm((DEPTH, R_HEADS, R_HEAD_DIM), 0.01),
        'w_branch': nrm((DEPTH, N_BRANCH, BRANCH_W, D_MODEL), BRANCH_W ** -0.5),
        'w_out': nrm((DEPTH, D_MODEL, D_MODEL), D_MODEL ** -0.5),
        'norm_cross': 1.0 + nrm((DEPTH, D_MODEL), 0.01),
        'w_xq': nrm((DEPTH, D_MODEL, X_W), D_MODEL ** -0.5),
        'w_xk': nrm((DEPTH, D_MODEL, X_W), D_MODEL ** -0.5),
        'w_xv': nrm((DEPTH, D_MODEL, X_W), D_MODEL ** -0.5),
        'w_xo': nrm((DEPTH, X_W, D_MODEL), X_W ** -0.5),
        'norm_ffn': 1.0 + nrm((DEPTH, D_MODEL), 0.01),
        'w_up': nrm((DEPTH, D_MODEL, D_FF), D_MODEL ** -0.5),
        'w_down': nrm((DEPTH, D_FF, D_MODEL), D_FF ** -0.5),
        'norm_final': 1.0 + nrm((D_MODEL,), 0.01),
    }


def reference(x_prompt, x_sample, mem_prompt, cache_win_k, cache_win_v, state_conv, state_lru, state_ret,
              cache_mem_k, cache_mem_v, norm_mix, w_in, attn_sink, conv_w, conv_b, lru_wa, lru_ba, lru_wx, lru_bx,
              lru_lambda, ret_gn, w_branch, w_out, norm_cross, w_xq, w_xk, w_xv, w_xo, norm_ffn, w_up, w_down,
              norm_final):
    bp, tp, _ = x_prompt.shape
    ts = x_sample.shape[1]
    mlen = mem_prompt.shape[1]
    pos_p = jnp.arange(tp)
    pos_s = PAST_LEN + jnp.arange(ts)
    conv0 = jnp.zeros((bp, CONV_WIDTH - 1, LRU_WIDTH), x_prompt.dtype)
    lru0 = jnp.zeros((bp, LRU_WIDTH), F32)
    ret0 = jnp.zeros((bp, R_HEADS, R_HEAD_DIM, R_HEAD_DIM), F32)
    hp, hs = x_prompt, x_sample
    p_wk, p_wv, p_conv, p_lru, p_ret, p_mk, p_mv = [], [], [], [], [], [], []
    s_wk, s_wv, s_conv, s_lru, s_ret = [], [], [], [], []
    for l in range(DEPTH):
        lw = (norm_mix[l], w_in[l], attn_sink[l], conv_w[l], conv_b[l], lru_wa[l], lru_ba[l], lru_wx[l],
              lru_bx[l], lru_lambda[l], ret_gn[l], w_branch[l], w_out[l], norm_cross[l], w_xq[l], w_xo[l],
              norm_ffn[l], w_up[l], w_down[l])
        mk = (mem_prompt @ w_xk[l]).reshape(bp, mlen, X_HEADS, X_HEAD_DIM)
        mv = (mem_prompt @ w_xv[l]).reshape(bp, mlen, X_HEADS, X_HEAD_DIM)
        hp, wk, wv, cv, hl, sr = trunk_layer(hp, pos_p, mk, mv, None, None, conv0, lru0, ret0, True, *lw)
        p_wk.append(wk); p_wv.append(wv); p_conv.append(cv); p_lru.append(hl); p_ret.append(sr)
        p_mk.append(mk); p_mv.append(mv)
        hs, wk, wv, cv, hl, sr = trunk_layer(hs, pos_s, cache_mem_k[l], cache_mem_v[l], cache_win_k[l],
                                             cache_win_v[l], state_conv[l], state_lru[l], state_ret[l], False, *lw)
        s_wk.append(wk); s_wv.append(wv); s_conv.append(cv); s_lru.append(hl); s_ret.append(sr)
    y_prompt = rmsnorm(hp, norm_final)
    y_sample = rmsnorm(hs, norm_final)
    return (y_prompt, y_sample,
            jnp.stack(p_wk), jnp.stack(p_wv), jnp.stack(p_conv), jnp.stack(p_lru), jnp.stack(p_ret),
            jnp.stack(p_mk), jnp.stack(p_mv),
            jnp.stack(s_wk), jnp.stack(s_wv), jnp.stack(s_conv), jnp.stack(s_lru), jnp.stack(s_ret))
```

```python
import functools

import jax
import jax.numpy as jnp
from jax import lax
from jax.experimental import pallas as pl
from jax.experimental.pallas import tpu as pltpu

F32 = jnp.float32
BF16 = jnp.bfloat16
NEG_INF = -1e30
EPS = 1e-6

V7X_VMEM_BYTES = 64 * 1024 * 1024
VMEM_LIMIT_BYTES = V7X_VMEM_BYTES - 8 * 1024 * 1024
LANES = 128
SUBLANES = 8

BRANCH_W = 1024
A_HEAD_DIM = 64
A_HEADS = 16
A_KV_HEADS = 4
A_GROUP = 4
A_KV_W = 256
WINDOW = 128
PAST_LEN = 8192
LRU_C = 8.0
CONV_WIDTH = 4
LRU_GROUP_W = 256
R_HEADS = 4
R_HEAD_DIM = 256
RET_CHUNK = 128
RET_THETA = 10000.0
X_HEADS = 4
X_HEAD_DIM = 128
X_W = 512

COL_QA, COL_XR, COL_YR, COL_QC, COL_KC, COL_VC, COL_GC, COL_GATES, COL_KA, COL_VA = (
    0, 1024, 2048, 3072, 4096, 5120, 6144, 7168, 13312, 13568)
IN_W = 13824


def _params(semantics):
    return pltpu.CompilerParams(dimension_semantics=semantics, vmem_limit_bytes=VMEM_LIMIT_BYTES)


def _smem_spec():
    return pl.BlockSpec(memory_space=pltpu.SMEM)


def _norm_matmul_body(x_ref, g_ref, w_ref, o_ref, u_ref, *, norm, relu2):
    @pl.when(pl.program_id(1) == 0)
    def _():
        x = x_ref[...]
        if norm:
            x = x * lax.rsqrt(jnp.mean(x * x, axis=-1, keepdims=True) + EPS)
            x = x * g_ref[...]
        u_ref[...] = x.astype(BF16)

    acc = jnp.dot(u_ref[...], w_ref[...], preferred_element_type=F32)
    if relu2:
        acc = jnp.maximum(acc, 0.0)
        acc = acc * acc
    o_ref[...] = acc.astype(o_ref.dtype)


def _norm_matmul(x, g, w, *, out_dtype, tn, norm=True, relu2=False, name):
    m, k = x.shape
    n = w.shape[1]
    tm = min(m, 1024)
    assert m % tm == 0 and n % tn == 0
    return pl.pallas_call(
        functools.partial(_norm_matmul_body, norm=norm, relu2=relu2),
        out_shape=jax.ShapeDtypeStruct((m, n), out_dtype),
        grid=(m // tm, n // tn),
        in_specs=[pl.BlockSpec((tm, k), lambda i, j: (i, 0)),
                  pl.BlockSpec((1, k), lambda i, j: (0, 0)),
                  pl.BlockSpec((k, tn), lambda i, j: (0, j))],
        out_specs=pl.BlockSpec((tm, tn), lambda i, j: (i, j)),
        scratch_shapes=[pltpu.VMEM((tm, k), BF16)],
        compiler_params=_params(("parallel", "arbitrary")),
        name=name,
    )(x, g.reshape(1, k).astype(F32), w)


def _matmul_residual_body(a_ref, w_ref, r_ref, o_ref, acc_ref, *, nk):
    kk = pl.program_id(2)
    part = jnp.dot(a_ref[...], w_ref[...], preferred_element_type=F32)
    if nk == 1:
        o_ref[...] = r_ref[...] + part
        return

    @pl.when(kk == 0)
    def _():
        acc_ref[...] = part

    @pl.when(kk > 0)
    def _():
        acc_ref[...] += part

    @pl.when(kk == nk - 1)
    def _():
        o_ref[...] = r_ref[...] + acc_ref[...]


def _matmul_residual(a, w, res, *, name):
    m, k = a.shape
    n = w.shape[1]
    tm = min(m, 1024)
    tn = 512
    tk = min(k, 2048)
    assert m % tm == 0 and n % tn == 0 and k % tk == 0
    nk = k // tk
    return pl.pallas_call(
        functools.partial(_matmul_residual_body, nk=nk),
        out_shape=jax.ShapeDtypeStruct((m, n), F32),
        grid=(m // tm, n // tn, nk),
        in_specs=[pl.BlockSpec((tm, tk), lambda i, j, kk: (i, kk)),
                  pl.BlockSpec((tk, tn), lambda i, j, kk: (kk, j)),
                  pl.BlockSpec((tm, tn), lambda i, j, kk: (i, j))],
        out_specs=pl.BlockSpec((tm, tn), lambda i, j, kk: (i, j)),
        scratch_shapes=[pltpu.VMEM((tm, tn), F32)],
        compiler_params=_params(("parallel", "parallel", "arbitrary")),
        name=name,
    )(a, w, res)


def _merge_body(oa_ref, ob_ref, oc_ref, w_ref, ga_ref, gb_ref, gc_ref, o_ref):
    acc = None
    for b, (o_b, g_b) in enumerate(((oa_ref, ga_ref), (ob_ref, gb_ref), (oc_ref, gc_ref))):
        proj = jnp.dot(o_b[...], w_ref[b], preferred_element_type=F32)
        term = jax.nn.sigmoid(g_b[...].astype(F32)) * proj
        acc = term if acc is None else acc + term
    o_ref[...] = acc.astype(o_ref.dtype)


def _merge(oa, ob, oc, w_branch, proj, *, name):
    m = oa.shape[0]
    d = w_branch.shape[2]
    tm = min(m, 1024)
    tn = 512
    g0 = COL_GATES // tn
    gstep = d // tn
    o_spec = pl.BlockSpec((tm, BRANCH_W), lambda i, j: (i, 0))

    def gate_spec(b):
        return pl.BlockSpec((tm, tn), lambda i, j: (i, g0 + b * gstep + j))

    return pl.pallas_call(
        _merge_body,
        out_shape=jax.ShapeDtypeStruct((m, d), BF16),
        grid=(m // tm, d // tn),
        in_specs=[o_spec, o_spec, o_spec,
                  pl.BlockSpec((3, BRANCH_W, tn), lambda i, j: (0, 0, j)),
                  gate_spec(0), gate_spec(1), gate_spec(2)],
        out_specs=pl.BlockSpec((tm, tn), lambda i, j: (i, j)),
        compiler_params=_params(("parallel", "parallel")),
        name=name,
    )(oa, ob, oc, w_branch, proj, proj, proj)


def _rmsnorm_body(x_ref, g_ref, o_ref):
    x = x_ref[...]
    y = x * lax.rsqrt(jnp.mean(x * x, axis=-1, keepdims=True) + EPS)
    o_ref[...] = y * g_ref[...]


def _rmsnorm(x, g, *, name):
    m, k = x.shape
    tm = min(m, 512)
    return pl.pallas_call(
        _rmsnorm_body,
        out_shape=jax.ShapeDtypeStruct((m, k), F32),
        grid=(m // tm,),
        in_specs=[pl.BlockSpec((tm, k), lambda i: (i, 0)), pl.BlockSpec((1, k), lambda i: (0, 0))],
        out_specs=pl.BlockSpec((tm, k), lambda i: (i, 0)),
        compiler_params=_params(("parallel",)),
        name=name,
    )(x, g.reshape(1, k).astype(F32))


def _sink_attention(q, k, v, mask, sink):
    s = lax.dot_general(q, k, (((1,), (1,)), ((), ())), preferred_element_type=F32)
    s = jnp.where(mask, s * (A_HEAD_DIM ** -0.5), NEG_INF)
    m = jnp.maximum(jnp.max(s, axis=-1, keepdims=True), sink)
    p = jnp.exp(s - m)
    denom = jnp.sum(p, axis=-1, keepdims=True) + jnp.exp(sink - m)
    o = jnp.dot(p.astype(BF16), v, preferred_element_type=F32)
    return o / denom


def _swa_prompt_body(sink_ref, q_ref, kp_ref, kc_ref, vp_ref, vc_ref, o_ref):
    n = pl.program_id(1)
    q = q_ref[...]
    k = jnp.concatenate([kp_ref[...], kc_ref[...]], axis=0)
    v = jnp.concatenate([vp_ref[...], vc_ref[...]], axis=0)
    row = lax.broadcasted_iota(jnp.int32, (WINDOW, 2 * WINDOW), 0)
    rel = lax.broadcasted_iota(jnp.int32, (WINDOW, 2 * WINDOW), 1) - WINDOW
    first_key = jnp.where(n > 0, -WINDOW, 0)
    mask = (rel <= row) & (rel > row - WINDOW) & (rel >= first_key)
    for h in range(A_KV_HEADS):
        kh = k[:, h * A_HEAD_DIM:(h + 1) * A_HEAD_DIM]
        vh = v[:, h * A_HEAD_DIM:(h + 1) * A_HEAD_DIM]
        outs = []
        for g in range(A_GROUP):
            hh = h * A_GROUP + g
            qh = q[:, hh * A_HEAD_DIM:(hh + 1) * A_HEAD_DIM]
            outs.append(_sink_attention(qh, kh, vh, mask, sink_ref[hh]))
        o_ref[:, h * A_KV_W:(h + 1) * A_KV_W] = jnp.concatenate(outs, axis=1).astype(o_ref.dtype)


def _swa_prompt(proj, sink, batch, seq, *, name):
    nb = seq // WINDOW
    ka, va = COL_KA // A_KV_W, COL_VA // A_KV_W
    kv = (WINDOW, A_KV_W)
    return pl.pallas_call(
        _swa_prompt_body,
        out_shape=jax.ShapeDtypeStruct((batch * seq, BRANCH_W), BF16),
        grid=(batch, nb),
        in_specs=[_smem_spec(),
                  pl.BlockSpec((WINDOW, BRANCH_W), lambda b, n: (b * nb + n, COL_QA // BRANCH_W)),
                  pl.BlockSpec(kv, lambda b, n: (b * nb + jnp.maximum(n - 1, 0), ka)),
                  pl.BlockSpec(kv, lambda b, n: (b * nb + n, ka)),
                  pl.BlockSpec(kv, lambda b, n: (b * nb + jnp.maximum(n - 1, 0), va)),
                  pl.BlockSpec(kv, lambda b, n: (b * nb + n, va))],
        out_specs=pl.BlockSpec((WINDOW, BRANCH_W), lambda b, n: (b * nb + n, 0)),
        compiler_params=_params(("parallel", "parallel")),
        name=name,
    )(sink, proj, proj, proj, proj, proj)


SAMPLE_T = 8
SWA_SAMPLE_G = 8
KEYS_PAD = WINDOW + 2 * SAMPLE_T


def _swa_sample_body(sink_ref, q_ref, kn_ref, vn_ref, ck_ref, cv_ref, o_ref, cko_ref, cvo_ref, o_sc):
    qf = q_ref[...].astype(F32)
    knf = kn_ref[...].astype(F32)
    vnf = vn_ref[...].astype(F32)
    rows = A_GROUP * SAMPLE_T
    t_idx = lax.broadcasted_iota(jnp.int32, (rows, KEYS_PAD), 0) % SAMPLE_T
    rel = lax.broadcasted_iota(jnp.int32, (rows, KEYS_PAD), 1) - WINDOW
    mask = (rel <= t_idx) & (rel > t_idx - WINDOW) & (rel < SAMPLE_T)
    grp = lax.broadcasted_iota(jnp.int32, (rows, 1), 0) // SAMPLE_T
    pad = jnp.zeros((SAMPLE_T, A_KV_W), F32)
    for g in range(SWA_SAMPLE_G):
        r0 = g * SAMPLE_T
        kn_g = knf[r0:r0 + SAMPLE_T]
        vn_g = vnf[r0:r0 + SAMPLE_T]
        ck = ck_ref[g]
        cv = cv_ref[g]
        cko_ref[g] = jnp.concatenate([ck[SAMPLE_T:], kn_g], axis=0)
        cvo_ref[g] = jnp.concatenate([cv[SAMPLE_T:], vn_g], axis=0)
        kk = jnp.concatenate([ck, kn_g, pad], axis=0).astype(BF16)
        vv = jnp.concatenate([cv, vn_g, pad], axis=0).astype(BF16)
        q_g = qf[r0:r0 + SAMPLE_T]
        for h in range(A_KV_HEADS):
            q4 = jnp.concatenate(
                [q_g[:, (h * A_GROUP + gg) * A_HEAD_DIM:(h * A_GROUP + gg + 1) * A_HEAD_DIM]
                 for gg in range(A_GROUP)], axis=0).astype(BF16)
            sink = jnp.zeros((rows, 1), F32)
            for gg in range(A_GROUP):
                sink = jnp.where(grp == gg, sink_ref[h * A_GROUP + gg], sink)
            o4 = _sink_attention(q4, kk[:, h * A_HEAD_DIM:(h + 1) * A_HEAD_DIM],
                                 vv[:, h * A_HEAD_DIM:(h + 1) * A_HEAD_DIM], mask, sink)
            o_sc[r0:r0 + SAMPLE_T, h * A_KV_W:(h + 1) * A_KV_W] = jnp.concatenate(
                [o4[gg * SAMPLE_T:(gg + 1) * SAMPLE_T] for gg in range(A_GROUP)], axis=1)
    o_ref[...] = o_sc[...].astype(o_ref.dtype)


def _swa_sample(proj, sink, cache_k, cache_v, *, name):
    nseq = cache_k.shape[0]
    g = SWA_SAMPLE_G
    rows = g * SAMPLE_T
    ka, va = COL_KA // A_KV_W, COL_VA // A_KV_W
    cache_spec = pl.BlockSpec((g, WINDOW, A_KV_W), lambda i: (i, 0, 0))
    cache_shape = jax.ShapeDtypeStruct((nseq, WINDOW, A_KV_W), F32)
    return pl.pallas_call(
        _swa_sample_body,
        out_shape=(jax.ShapeDtypeStruct((nseq * SAMPLE_T, BRANCH_W), BF16), cache_shape, cache_shape),
        grid=(nseq // g,),
        in_specs=[_smem_spec(),
                  pl.BlockSpec((rows, BRANCH_W), lambda i: (i, COL_QA // BRANCH_W)),
                  pl.BlockSpec((rows, A_KV_W), lambda i: (i, ka)),
                  pl.BlockSpec((rows, A_KV_W), lambda i: (i, va)),
                  cache_spec, cache_spec],
        out_specs=(pl.BlockSpec((rows, BRANCH_W), lambda i: (i, 0)), cache_spec, cache_spec),
        scratch_shapes=[pltpu.VMEM((rows, BRANCH_W), F32)],
        compiler_params=_params(("parallel",)),
        name=name,
    )(sink, proj, proj, proj, cache_k, cache_v)


def _lru_gates(xc, wa_ref, ba_ref, wx_ref, bx_ref, lam_ref):
    xb = xc.astype(BF16)
    ngroups, gw, _ = wa_ref.shape

    def blockdiag(w_ref):
        return jnp.concatenate(
            [jnp.dot(xb[:, c * gw:(c + 1) * gw], w_ref[c], preferred_element_type=F32)
             for c in range(ngroups)], axis=1)

    r = jax.nn.sigmoid(blockdiag(wa_ref) + ba_ref[...])
    i = jax.nn.sigmoid(blockdiag(wx_ref) + bx_ref[...])
    nl = -lam_ref[...]
    softplus = jnp.maximum(nl, 0.0) + jnp.log1p(jnp.exp(-jnp.abs(nl)))
    log_a = (-LRU_C * r) * softplus
    a = jnp.exp(log_a)
    th = jnp.tanh(log_a)
    mult = jnp.sqrt((-2.0 * th) / (1.0 - th))
    return a, mult, i


LRU_ROWS = 256


def _lru_prompt_body(x_ref, y_ref, cw_ref, cb_ref, wa_ref, ba_ref, wx_ref, bx_ref, lam_ref,
                     o_ref, hlast_ref, xs_ref, a_ref, u_ref, h_ref):
    t = pl.program_id(1)
    rows, width = x_ref.shape

    @pl.when(t == 0)
    def _():
        xs_ref[0:SUBLANES, :] = jnp.zeros((SUBLANES, width), F32)
        h_ref[...] = jnp.zeros((SUBLANES, width), F32)

    x = x_ref[...].astype(F32)
    xs_ref[SUBLANES:SUBLANES + rows, :] = x
    cw = cw_ref[...]
    taps = [xs_ref[SUBLANES - 3 + j:SUBLANES - 3 + j + rows, :] * cw[j:j + 1] for j in range(CONV_WIDTH - 1)]
    taps.append(x * cw[CONV_WIDTH - 1:CONV_WIDTH])
    xc = cb_ref[...] + (((taps[0] + taps[1]) + taps[2]) + taps[3])
    xs_ref[0:SUBLANES, :] = xs_ref[rows:rows + SUBLANES, :]

    a, mult, i = _lru_gates(xc, wa_ref, ba_ref, wx_ref, bx_ref, lam_ref)
    first = (lax.broadcasted_iota(jnp.int32, (rows, 1), 0) == 0) & (t == 0)
    mult = jnp.where(first, 1.0, mult)
    a_ref[...] = a
    u_ref[...] = mult * (i * xc)

    rid = lax.broadcasted_iota(jnp.int32, (SUBLANES, width), 0)

    def tile(j, h):
        off = pl.multiple_of(j * SUBLANES, SUBLANES)
        at = a_ref[pl.ds(off, SUBLANES), :]
        ut = u_ref[pl.ds(off, SUBLANES), :]
        for d in (1, 2, 4):
            keep = rid >= d
            a_sh = jnp.where(keep, pltpu.roll(at, d, 0), 1.0)
            u_sh = jnp.where(keep, pltpu.roll(ut, d, 0), 0.0)
            ut = at * u_sh + ut
            at = at * a_sh
        hh = at * h + ut
        u_ref[pl.ds(off, SUBLANES), :] = hh
        return jnp.broadcast_to(hh[SUBLANES - 1:SUBLANES, :], (SUBLANES, width))

    h = lax.fori_loop(0, rows // SUBLANES, tile, h_ref[...], unroll=2)
    h_ref[...] = h
    hlast_ref[0] = h[0:1, :]
    o_ref[...] = (jax.nn.gelu(y_ref[...].astype(F32)) * u_ref[...]).astype(o_ref.dtype)


def _lru_prompt(proj, batch, seq, conv_w, conv_b, wa, ba, wx, bx, lam, *, name):
    rows = min(LRU_ROWS, seq)
    nt = seq // rows
    w = BRANCH_W
    vec = pl.BlockSpec((1, w), lambda b, t: (0, 0))
    gate_w = pl.BlockSpec((w // LRU_GROUP_W, LRU_GROUP_W, LRU_GROUP_W), lambda b, t: (0, 0, 0))
    return pl.pallas_call(
        _lru_prompt_body,
        out_shape=(jax.ShapeDtypeStruct((batch * seq, w), BF16), jax.ShapeDtypeStruct((batch, 1, w), F32)),
        grid=(batch, nt),
        in_specs=[pl.BlockSpec((rows, w), lambda b, t: (b * nt + t, COL_XR // w)),
                  pl.BlockSpec((rows, w), lambda b, t: (b * nt + t, COL_YR // w)),
                  pl.BlockSpec((CONV_WIDTH, w), lambda b, t: (0, 0)),
                  vec, gate_w, vec, gate_w, vec, vec],
        out_specs=(pl.BlockSpec((rows, w), lambda b, t: (b * nt + t, 0)),
                   pl.BlockSpec((1, 1, w), lambda b, t: (b, 0, 0))),
        scratch_shapes=[pltpu.VMEM((rows + SUBLANES, w), F32), pltpu.VMEM((rows, w), F32),
                        pltpu.VMEM((rows, w), F32), pltpu.VMEM((SUBLANES, w), F32)],
        compiler_params=_params(("parallel", "arbitrary")),
        name=name,
    )(proj, proj, conv_w, conv_b.reshape(1, w), wa, ba.reshape(1, w), wx, bx.reshape(1, w), lam.reshape(1, w))


def _lru_sample_body(x_ref, y_ref, c0_ref, c1_ref, c2_ref, h0_ref, cw_ref, cb_ref, wa_ref, ba_ref,
                     wx_ref, bx_ref, lam_ref, o_ref, c0o_ref, c1o_ref, c2o_ref, ho_ref, xs_ref, ys_ref, os_ref):
    nseq = h0_ref.shape[0]
    xs_ref[...] = x_ref[...].astype(F32)
    ys_ref[...] = y_ref[...].astype(F32)

    def step(ref, t):
        return ref[pl.ds(t, nseq, stride=SAMPLE_T), :]

    hist = [c0_ref[...], c1_ref[...], c2_ref[...]] + [step(xs_ref, t) for t in range(SAMPLE_T)]
    cw = cw_ref[...]
    xcs = []
    for t in range(SAMPLE_T):
        taps = [hist[t + j] * cw[j:j + 1] for j in range(CONV_WIDTH)]
        xcs.append(cb_ref[...] + (((taps[0] + taps[1]) + taps[2]) + taps[3]))
    xc = jnp.concatenate(xcs, axis=0)
    a, mult, i = _lru_gates(xc, wa_ref, ba_ref, wx_ref, bx_ref, lam_ref)
    u = mult * (i * xc)
    h = h0_ref[...]
    for t in range(SAMPLE_T):
        h = a[t * nseq:(t + 1) * nseq] * h + u[t * nseq:(t + 1) * nseq]
        os_ref[pl.ds(t, nseq, stride=SAMPLE_T), :] = jax.nn.gelu(step(ys_ref, t)) * h
    o_ref[...] = os_ref[...].astype(o_ref.dtype)
    ho_ref[...] = h
    c0o_ref[...] = hist[SAMPLE_T]
    c1o_ref[...] = hist[SAMPLE_T + 1]
    c2o_ref[...] = hist[SAMPLE_T + 2]


def _lru_sample(proj, state_conv, state_lru, conv_w, conv_b, wa, ba, wx, bx, lam, *, name):
    nseq, w = state_lru.shape
    rows = nseq * SAMPLE_T
    cw = LANES
    nc = w // cw
    assert wa.shape == (nc, cw, cw)
    conv_flat = state_conv.reshape(nseq, (CONV_WIDTH - 1) * w)

    def conv_spec(j):
        return pl.BlockSpec((nseq, cw), lambda c: (0, j * nc + c))

    vec = pl.BlockSpec((1, cw), lambda c: (0, c))
    gate_w = pl.BlockSpec((1, cw, cw), lambda c: (c, 0, 0))
    state = jax.ShapeDtypeStruct((nseq, w), F32)
    state_spec = pl.BlockSpec((nseq, cw), lambda c: (0, c))
    outs = pl.pallas_call(
        _lru_sample_body,
        out_shape=(jax.ShapeDtypeStruct((rows, w), BF16), state, state, state, state),
        grid=(nc,),
        in_specs=[pl.BlockSpec((rows, cw), lambda c: (0, COL_XR // cw + c)),
                  pl.BlockSpec((rows, cw), lambda c: (0, COL_YR // cw + c)),
                  conv_spec(0), conv_spec(1), conv_spec(2), state_spec,
                  pl.BlockSpec((CONV_WIDTH, cw), lambda c: (0, c)),
                  vec, gate_w, vec, gate_w, vec, vec],
        out_specs=(pl.BlockSpec((rows, cw), lambda c: (0, c)), state_spec, state_spec, state_spec, state_spec),
        scratch_shapes=[pltpu.VMEM((rows, cw), F32), pltpu.VMEM((rows, cw), F32), pltpu.VMEM((rows, cw), F32)],
        compiler_params=_params(("parallel",)),
        name=name,
    )(proj, proj, conv_flat, conv_flat, conv_flat, state_lru, conv_w, conv_b.reshape(1, w), wa,
      ba.reshape(1, w), wx, bx.reshape(1, w), lam.reshape(1, w))
    ob, c0, c1, c2, h = outs
    return ob, jnp.stack([c0, c1, c2], axis=1), h


def _rotate_pairs(x, cos, sin_signed, even_lane):
    width = x.shape[1]
    partner = jnp.where(even_lane, pltpu.roll(x, width - 1, 1), pltpu.roll(x, 1, 1))
    return x * cos + partner * sin_signed


def _retention_head(qh, kh, vh, s_prev, dmat, qdec, kdec, cdec):
    inner = lax.dot_general(qh.astype(BF16), kh.astype(BF16), (((1,), (1,)), ((), ())),
                            preferred_element_type=F32) * dmat
    o = jnp.dot(inner.astype(BF16), vh, preferred_element_type=F32)
    o = o + jnp.dot((qh * qdec).astype(BF16), s_prev.astype(BF16), preferred_element_type=F32)
    s_new = cdec * s_prev + lax.dot_general((kh * kdec).astype(BF16), vh, (((0,), (0,)), ((), ())),
                                            preferred_element_type=F32)
    return o, s_new


def _group_norm_gate(o, gate, gain):
    mu = jnp.mean(o, axis=-1, keepdims=True)
    oc = o - mu
    var = jnp.mean(oc * oc, axis=-1, keepdims=True)
    return jax.nn.silu(gate) * (oc * lax.rsqrt(var + EPS) * gain)


def _ret_prompt_body(cdec_ref, q_ref, k_ref, v_ref, g_ref, cos_ref, sin_ref, dmat_ref, qdec_ref, kdec_ref,
                     gn_ref, o_ref, s_ref):
    @pl.when(pl.program_id(1) == 0)
    def _():
        s_ref[...] = jnp.zeros(s_ref.shape, F32)

    rows, width = q_ref.shape
    even = (lax.broadcasted_iota(jnp.int32, (rows, width), 1) & 1) == 0
    cos = jnp.concatenate([cos_ref[...]] * R_HEADS, axis=1)
    sin = jnp.concatenate([sin_ref[...]] * R_HEADS, axis=1)
    qr = _rotate_pairs(q_ref[...].astype(F32), cos, sin, even)
    kr = _rotate_pairs(k_ref[...].astype(F32), cos, sin, even) * (R_HEAD_DIM ** -0.5)
    v = v_ref[...]
    gate = g_ref[...].astype(F32)
    qdec = qdec_ref[...]
    kdec = kdec_ref[...]
    gn = gn_ref[...]
    for h in range(R_HEADS):
        sl = slice(h * R_HEAD_DIM, (h + 1) * R_HEAD_DIM)
        o, s_new = _retention_head(qr[:, sl], kr[:, sl], v[:, sl], s_ref[0, h], dmat_ref[h],
                                   qdec[:, sl], kdec[:, sl], cdec_ref[h])
        s_ref[0, h] = s_new
        o_ref[:, sl] = _group_norm_gate(o, gate[:, sl], gn[:, sl]).astype(o_ref.dtype)


def _ret_tables(pos, chunk):
    half = R_HEAD_DIM // 2
    inv = 1.0 / (RET_THETA ** jnp.linspace(0.0, 1.0, half, dtype=F32))
    ang = pos.astype(F32)[:, None] * inv[None, :]
    cos = jnp.repeat(jnp.cos(ang), 2, axis=1)
    sin = jnp.sin(ang)
    sin_signed = jnp.stack([-sin, sin], axis=-1).reshape(pos.shape[0], R_HEAD_DIM)
    lg = jnp.log1p(-jnp.exp2(-5.0 - jnp.arange(R_HEADS, dtype=F32)))
    n = jnp.arange(chunk, dtype=F32)
    diff = n[:, None] - n[None, :]
    dmat = jnp.where(diff >= 0, jnp.exp(jnp.maximum(diff, 0.0)[None] * lg[:, None, None]), 0.0)
    qdec = jnp.repeat(jnp.exp((n[:, None] + 1.0) * lg[None, :]), R_HEAD_DIM, axis=1)
    kdec = jnp.repeat(jnp.exp((chunk - 1.0 - n)[:, None] * lg[None, :]), R_HEAD_DIM, axis=1)
    cdec = jnp.exp(chunk * lg)
    return cos, sin_signed, dmat, qdec, kdec, cdec


def _ret_prompt(proj, batch, seq, ret_gn, *, name):
    c = RET_CHUNK
    nc = seq // c
    w = BRANCH_W
    cos, sin, dmat, qdec, kdec, cdec = _ret_tables(jnp.arange(seq), c)

    def col(off):
        return pl.BlockSpec((c, w), lambda b, n: (b * nc + n, off // w))

    const2 = lambda b, n: (0, 0)
    return pl.pallas_call(
        _ret_prompt_body,
        out_shape=(jax.ShapeDtypeStruct((batch * seq, w), BF16),
                   jax.ShapeDtypeStruct((batch, R_HEADS, R_HEAD_DIM, R_HEAD_DIM), F32)),
        grid=(batch, nc),
        in_specs=[_smem_spec(), col(COL_QC), col(COL_KC), col(COL_VC), col(COL_GC),
                  pl.BlockSpec((c, R_HEAD_DIM), lambda b, n: (n, 0)),
                  pl.BlockSpec((c, R_HEAD_DIM), lambda b, n: (n, 0)),
                  pl.BlockSpec((R_HEADS, c, c), lambda b, n: (0, 0, 0)),
                  pl.BlockSpec((c, w), const2), pl.BlockSpec((c, w), const2), pl.BlockSpec((1, w), const2)],
        out_specs=(pl.BlockSpec((c, w), lambda b, n: (b * nc + n, 0)),
                   pl.BlockSpec((1, R_HEADS, R_HEAD_DIM, R_HEAD_DIM), lambda b, n: (b, 0, 0, 0))),
        compiler_params=_params(("parallel", "arbitrary")),
        name=name,
    )(cdec, proj, proj, proj, proj, cos, sin, dmat, qdec, kdec, ret_gn.reshape(1, w))


RET_SAMPLE_G = 4


def _ret_sample_body(cdec_ref, q_ref, k_ref, v_ref, g_ref, cos_ref, sin_ref, dmat_ref, qdec_ref, kdec_ref,
                     gn_ref, s_ref, o_ref, so_ref, o_sc):
    rows, width = q_ref.shape
    even = (lax.broadcasted_iota(jnp.int32, (rows, width), 1) & 1) == 0
    cos = cos_ref[...]
    sin = sin_ref[...]
    qr = _rotate_pairs(q_ref[...].astype(F32), cos, sin, even)
    kr = _rotate_pairs(k_ref[...].astype(F32), cos, sin, even) * (R_HEAD_DIM ** -0.5)
    vf = v_ref[...].astype(F32)
    gate = g_ref[...].astype(F32)
    qdec = qdec_ref[...]
    kdec = kdec_ref[...]
    gn = gn_ref[...]
    for g in range(RET_SAMPLE_G):
        rs = slice(g * SAMPLE_T, (g + 1) * SAMPLE_T)
        for h in range(R_HEADS):
            sl = slice(h * R_HEAD_DIM, (h + 1) * R_HEAD_DIM)
            o, s_new = _retention_head(qr[rs, sl], kr[rs, sl], vf[rs, sl].astype(BF16), s_ref[g, h],
                                       dmat_ref[h], qdec[rs, sl], kdec[rs, sl], cdec_ref[h])
            so_ref[g, h] = s_new
            o_sc[rs, sl] = _group_norm_gate(o, gate[rs, sl], gn[:, sl])
    o_ref[...] = o_sc[...].astype(o_ref.dtype)


def _ret_sample(proj, state, ret_gn, *, name):
    nseq = state.shape[0]
    g = RET_SAMPLE_G
    rows = g * SAMPLE_T
    w = BRANCH_W
    pos = PAST_LEN + jnp.arange(SAMPLE_T)
    cos, sin, dmat, qdec, kdec, cdec = _ret_tables(pos, SAMPLE_T)
    tile_rows = lambda x: jnp.tile(x, (g, 1))
    cos = tile_rows(jnp.tile(cos, (1, R_HEADS)))
    sin = tile_rows(jnp.tile(sin, (1, R_HEADS)))
    qdec, kdec = tile_rows(qdec), tile_rows(kdec)

    def col(off):
        return pl.BlockSpec((rows, w), lambda i: (i, off // w))

    const2 = lambda i: (0, 0)
    tab = pl.BlockSpec((rows, w), const2)
    state_spec = pl.BlockSpec((g, R_HEADS, R_HEAD_DIM, R_HEAD_DIM), lambda i: (i, 0, 0, 0))
    return pl.pallas_call(
        _ret_sample_body,
        out_shape=(jax.ShapeDtypeStruct((nseq * SAMPLE_T, w), BF16), jax.ShapeDtypeStruct(state.shape, F32)),
        grid=(nseq // g,),
        in_specs=[_smem_spec(), col(COL_QC), col(COL_KC), col(COL_VC), col(COL_GC), tab, tab,
                  pl.BlockSpec((R_HEADS, SAMPLE_T, SAMPLE_T), lambda i: (0, 0, 0)),
                  tab, tab, pl.BlockSpec((1, w), const2), state_spec],
        out_specs=(pl.BlockSpec((rows, w), lambda i: (i, 0)), state_spec),
        scratch_shapes=[pltpu.VMEM((rows, w), F32)],
        compiler_params=_params(("parallel",)),
        name=name,
    )(cdec, proj, proj, proj, proj, cos, sin, dmat, qdec, kdec, ret_gn.reshape(1, w), state)


def _cross_head(q, k, v):
    s = lax.dot_general(q, k, (((1,), (1,)), ((), ())), preferred_element_type=F32) * (X_HEAD_DIM ** -0.5)
    p = jnp.exp(s - jnp.max(s, axis=-1, keepdims=True))
    denom = jnp.sum(p, axis=-1, keepdims=True)
    return jnp.dot(p.astype(BF16), v, preferred_element_type=F32) / denom


def _cross_prompt_body(q_ref, k_ref, v_ref, o_ref):
    q = q_ref[...]
    for h in range(X_HEADS):
        sl = slice(h * X_HEAD_DIM, (h + 1) * X_HEAD_DIM)
        o_ref[:, sl] = _cross_head(q[:, sl], k_ref[0, :, sl].astype(BF16),
                                   v_ref[0, :, sl].astype(BF16)).astype(o_ref.dtype)


def _cross_prompt(xq, mem_k, mem_v, batch, seq, *, name):
    tq = min(seq, 512)
    nq = seq // tq
    mem = mem_k.shape[1]
    kv = pl.BlockSpec((1, mem, X_W), lambda b, i: (b, 0, 0))
    return pl.pallas_call(
        _cross_prompt_body,
        out_shape=jax.ShapeDtypeStruct((batch * seq, X_W), BF16),
        grid=(batch, nq),
        in_specs=[pl.BlockSpec((tq, X_W), lambda b, i: (b * nq + i, 0)), kv, kv],
        out_specs=pl.BlockSpec((tq, X_W), lambda b, i: (b * nq + i, 0)),
        compiler_params=_params(("parallel", "parallel")),
        name=name,
    )(xq, mem_k, mem_v)


CROSS_SAMPLE_G = 8


def _cross_sample_body(q_ref, k_ref, v_ref, o_ref, o_sc):
    qf = q_ref[...].astype(F32)
    for g in range(CROSS_SAMPLE_G):
        rs = slice(g * SAMPLE_T, (g + 1) * SAMPLE_T)
        for h in range(X_HEADS):
            sl = slice(h * X_HEAD_DIM, (h + 1) * X_HEAD_DIM)
            o_sc[rs, sl] = _cross_head(qf[rs, sl].astype(BF16), k_ref[g, :, sl].astype(BF16),
                                       v_ref[g, :, sl].astype(BF16))
    o_ref[...] = o_sc[...].astype(o_ref.dtype)


def _cross_sample(xq, mem_k, mem_v, *, name):
    nseq, mem, _ = mem_k.shape
    g = CROSS_SAMPLE_G
    rows = g * SAMPLE_T
    kv = pl.BlockSpec((g, mem, X_W), lambda i: (i, 0, 0))
    return pl.pallas_call(
        _cross_sample_body,
        out_shape=jax.ShapeDtypeStruct((nseq * SAMPLE_T, X_W), BF16),
        grid=(nseq // g,),
        in_specs=[pl.BlockSpec((rows, X_W), lambda i: (i, 0)), kv, kv],
        out_specs=pl.BlockSpec((rows, X_W), lambda i: (i, 0)),
        scratch_shapes=[pltpu.VMEM((rows, X_W), F32)],
        compiler_params=_params(("parallel",)),
        name=name,
    )(xq, mem_k, mem_v)


def _block_diag_groups(w, group_w):
    depth, nb, bs, _ = w.shape
    per = group_w // bs
    w = w.reshape(depth, nb // per, per, bs, bs)
    eye = jnp.eye(per, dtype=w.dtype)
    return jnp.einsum('lcipq,ij->lcipjq', w, eye).reshape(depth, nb // per, group_w, group_w).astype(BF16)


def _dense_tail(h, merged, lw, xattn, tag):
    h = _matmul_residual(merged, lw['w_out'], h, name=f'out_proj_{tag}')
    xq = _norm_matmul(h, lw['norm_cross'], lw['w_xq'], out_dtype=BF16, tn=X_W, name=f'xq_{tag}')
    h = _matmul_residual(xattn(xq), lw['w_xo'], h, name=f'xo_{tag}')
    f = _norm_matmul(h, lw['norm_ffn'], lw['w_up'], out_dtype=BF16, tn=1024, relu2=True, name=f'up_{tag}')
    return _matmul_residual(f, lw['w_down'], h, name=f'down_{tag}')


def kernel(x_prompt, x_sample, mem_prompt, cache_win_k, cache_win_v, state_conv, state_lru, state_ret, cache_mem_k, cache_mem_v, norm_mix, w_in, attn_sink, conv_w, conv_b, lru_wa, lru_ba, lru_wx, lru_bx, lru_lambda, ret_gn, w_branch, w_out, norm_cross, w_xq, w_xk, w_xv, w_xo, norm_ffn, w_up, w_down, norm_final):
    bp, tp, d = x_prompt.shape
    bs, ts, _ = x_sample.shape
    depth = w_in.shape[0]
    mlen = mem_prompt.shape[1]
    wb = cache_win_k.shape[2]
    assert ts == SAMPLE_T and wb == WINDOW and w_in.shape[2] == IN_W
    assert tp % WINDOW == 0 and tp % RET_CHUNK == 0

    w_in_b = jnp.concatenate([w_in[..., :1024], w_in[..., 1536:], w_in[..., 1024:1536]], axis=-1).astype(BF16)
    bf = lambda w: w.astype(BF16)
    w_branch_b, w_out_b, w_xq_b, w_xk_b, w_xv_b, w_xo_b, w_up_b, w_down_b = map(
        bf, (w_branch, w_out, w_xq, w_xk, w_xv, w_xo, w_up, w_down))
    wa_p, wx_p = _block_diag_groups(lru_wa, LRU_GROUP_W), _block_diag_groups(lru_wx, LRU_GROUP_W)
    wa_s, wx_s = _block_diag_groups(lru_wa, LANES), _block_diag_groups(lru_wx, LANES)

    hp = x_prompt.reshape(bp * tp, d)
    hs = x_sample.reshape(bs * ts, d)
    mem2d = mem_prompt.reshape(bp * mlen, d)
    ones = jnp.ones((d,), F32)
    outs = {k: [] for k in ('p_wk', 'p_wv', 'p_conv', 'p_lru', 'p_ret', 'p_mk', 'p_mv',
                            's_wk', 's_wv', 's_conv', 's_lru', 's_ret')}
    for l in range(depth):
        lw = dict(w_out=w_out_b[l], norm_cross=norm_cross[l], w_xq=w_xq_b[l], w_xo=w_xo_b[l],
                  norm_ffn=norm_ffn[l], w_up=w_up_b[l], w_down=w_down_b[l])
        lru_w_p = (conv_w[l], conv_b[l], wa_p[l], lru_ba[l], wx_p[l], lru_bx[l], lru_lambda[l])
        lru_w_s = (conv_w[l], conv_b[l], wa_s[l], lru_ba[l], wx_s[l], lru_bx[l], lru_lambda[l])
        sink = attn_sink[l].astype(F32)

        mk = _norm_matmul(mem2d, ones, w_xk_b[l], out_dtype=F32, tn=X_W, norm=False, name=f'mem_k_{l}')
        mv = _norm_matmul(mem2d, ones, w_xv_b[l], out_dtype=F32, tn=X_W, norm=False, name=f'mem_v_{l}')
        mk3, mv3 = mk.reshape(bp, mlen, X_W), mv.reshape(bp, mlen, X_W)
        proj = _norm_matmul(hp, norm_mix[l], w_in_b[l], out_dtype=BF16, tn=1536, name=f'in_proj_p{l}')
        oa = _swa_prompt(proj, sink, bp, tp, name=f'swa_p{l}')
        ob, lru_last = _lru_prompt(proj, bp, tp, *lru_w_p, name=f'lru_p{l}')
        oc, ret_last = _ret_prompt(proj, bp, tp, ret_gn[l], name=f'ret_p{l}')
        merged = _merge(oa, ob, oc, w_branch_b[l], proj, name=f'merge_p{l}')
        hp = _dense_tail(hp, merged, lw,
                         lambda xq: _cross_prompt(xq, mk3, mv3, bp, tp, name=f'cross_p{l}'), f'p{l}')
        proj3 = proj.reshape(bp, tp, IN_W)
        outs['p_wk'].append(proj3[:, tp - wb:, COL_KA:COL_KA + A_KV_W].astype(F32).reshape(bp, wb, A_KV_HEADS, A_HEAD_DIM))
        outs['p_wv'].append(proj3[:, tp - wb:, COL_VA:COL_VA + A_KV_W].astype(F32).reshape(bp, wb, A_KV_HEADS, A_HEAD_DIM))
        outs['p_conv'].append(proj3[:, tp - (CONV_WIDTH - 1):, COL_XR:COL_XR + BRANCH_W].astype(F32))
        outs['p_lru'].append(lru_last.reshape(bp, BRANCH_W))
        outs['p_ret'].append(ret_last)
        outs['p_mk'].append(mk.reshape(bp, mlen, X_HEADS, X_HEAD_DIM))
        outs['p_mv'].append(mv.reshape(bp, mlen, X_HEADS, X_HEAD_DIM))

        proj = _norm_matmul(hs, norm_mix[l], w_in_b[l], out_dtype=BF16, tn=1536, name=f'in_proj_s{l}')
        oa, wk, wv = _swa_sample(proj, sink, cache_win_k[l].reshape(bs, wb, A_KV_W),
                                 cache_win_v[l].reshape(bs, wb, A_KV_W), name=f'swa_s{l}')
        ob, conv_new, lru_new = _lru_sample(proj, state_conv[l], state_lru[l], *lru_w_s, name=f'lru_s{l}')
        oc, ret_new = _ret_sample(proj, state_ret[l], ret_gn[l], name=f'ret_s{l}')
        merged = _merge(oa, ob, oc, w_branch_b[l], proj, name=f'merge_s{l}')
        cmk = cache_mem_k[l].reshape(bs, mlen, X_W)
        cmv = cache_mem_v[l].reshape(bs, mlen, X_W)
        hs = _dense_tail(hs, merged, lw, lambda xq: _cross_sample(xq, cmk, cmv, name=f'cross_s{l}'), f's{l}')
        outs['s_wk'].append(wk.reshape(bs, wb, A_KV_HEADS, A_HEAD_DIM))
        outs['s_wv'].append(wv.reshape(bs, wb, A_KV_HEADS, A_HEAD_DIM))
        outs['s_conv'].append(conv_new)
        outs['s_lru'].append(lru_new)
        outs['s_ret'].append(ret_new)

    y_prompt = _rmsnorm(hp, norm_final, name='final_norm_p').reshape(bp, tp, d)
    y_sample = _rmsnorm(hs, norm_final, name='final_norm_s').reshape(bs, ts, d)
    st = lambda k: jnp.stack(outs[k])
    return (y_prompt, y_sample, st('p_wk'), st('p_wv'), st('p_conv'), st('p_lru'), st('p_ret'),
            st('p_mk'), st('p_mv'), st('s_wk'), st('s_wv'), st('s_conv'), st('s_lru'), st('s_ret'))
```

```python
import functools

import jax
import jax.numpy as jnp
from jax import lax
from jax.experimental import pallas as pl
from jax.experimental.pallas import tpu as pltpu

F32 = jnp.float32
BF16 = jnp.bfloat16
NEG_INF = -1e30
EPS = 1e-6

V7X_VMEM_BYTES = 64 * 1024 * 1024
VMEM_LIMIT_BYTES = V7X_VMEM_BYTES - 8 * 1024 * 1024
LANES = 128
SUBLANES = 8

BRANCH_W = 1024
A_HEAD_DIM = 64
A_HEADS = 16
A_KV_HEADS = 4
A_GROUP = 4
A_KV_W = 256
WINDOW = 128
PAST_LEN = 8192
LRU_C = 8.0
CONV_WIDTH = 4
LRU_GROUP_W = 256
R_HEADS = 4
R_HEAD_DIM = 256
RET_CHUNK = 128
RET_THETA = 10000.0
X_HEADS = 4
X_HEAD_DIM = 128
X_W = 512

COL_QA, COL_XR, COL_YR, COL_QC, COL_KC, COL_VC, COL_GC, COL_GATES, COL_KA, COL_VA = (
    0, 1024, 2048, 3072, 4096, 5120, 6144, 7168, 13312, 13568)
IN_W = 13824
PERM_BLOCK = 512
ROW_TILE = 1024


def _params(semantics):
    return pltpu.CompilerParams(dimension_semantics=semantics, vmem_limit_bytes=VMEM_LIMIT_BYTES)


def _smem_spec():
    return pl.BlockSpec(memory_space=pltpu.SMEM)


def _rms(x, g):
    y = x * lax.rsqrt(jnp.mean(x * x, axis=-1, keepdims=True) + EPS)
    return y * g


def _layer_vec(v):
    return v.reshape(v.shape[0], 1, v.shape[1]).astype(F32)


def _rms_cast_body(x_ref, g_ref, o_ref):
    o_ref[...] = _rms(x_ref[...], g_ref[...]).astype(o_ref.dtype)


def _rms_cast(x, g, l, *, out_dtype, name):
    m, k = x.shape
    tm = min(m, 512)
    return pl.pallas_call(
        _rms_cast_body,
        out_shape=jax.ShapeDtypeStruct((m, k), out_dtype),
        grid=(m // tm,),
        in_specs=[pl.BlockSpec((tm, k), lambda i: (i, 0)), pl.BlockSpec((None, 1, k), lambda i: (l, 0, 0))],
        out_specs=pl.BlockSpec((tm, k), lambda i: (i, 0)),
        compiler_params=_params(("parallel",)),
        name=name,
    )(x, g)


def _in_proj_body(u_ref, w0_ref, w1_ref, w2_ref, o_ref, wb_ref):
    @pl.when(pl.program_id(1) == 0)
    def _():
        for k, w_ref in enumerate((w0_ref, w1_ref, w2_ref)):
            wb_ref[:, k * PERM_BLOCK:(k + 1) * PERM_BLOCK] = w_ref[...].astype(BF16)

    o_ref[...] = jnp.dot(u_ref[...], wb_ref[...], preferred_element_type=F32).astype(o_ref.dtype)


def _in_proj(u, w_in, l, *, name):
    m, k = u.shape
    tm = min(m, ROW_TILE)
    per = 3
    tn = per * PERM_BLOCK
    nblk = IN_W // PERM_BLOCK

    def src_block(c):
        return jnp.where(c < 2, c, jnp.where(c < nblk - 1, c + 1, 2))

    def w_spec(kk):
        return pl.BlockSpec((None, k, PERM_BLOCK), lambda j, i: (l, 0, src_block(per * j + kk)))

    return pl.pallas_call(
        _in_proj_body,
        out_shape=jax.ShapeDtypeStruct((m, IN_W), BF16),
        grid=(IN_W // tn, m // tm),
        in_specs=[pl.BlockSpec((tm, k), lambda j, i: (i, 0)), w_spec(0), w_spec(1), w_spec(2)],
        out_specs=pl.BlockSpec((tm, tn), lambda j, i: (i, j)),
        scratch_shapes=[pltpu.VMEM((k, tn), BF16)],
        compiler_params=_params(("parallel", "arbitrary")),
        name=name,
    )(u, w_in, w_in, w_in)


def _ws_matmul_body(*refs, norm, residual):
    it = iter(refs)
    x_ref = next(it)
    g_ref = next(it) if norm else None
    w_ref = next(it)
    r_ref = next(it) if residual else None
    o_ref = next(it)
    wb_ref = next(it)

    @pl.when(pl.program_id(1) == 0)
    def _():
        wb_ref[...] = w_ref[...].astype(BF16)

    x = x_ref[...]
    if norm:
        x = _rms(x, g_ref[...])
    acc = jnp.dot(x.astype(BF16), wb_ref[...], preferred_element_type=F32)
    if residual:
        acc = r_ref[...] + acc
    o_ref[...] = acc.astype(o_ref.dtype)


def _ws_matmul(x, w, l, *, tn, out_dtype, g=None, res=None, name):
    m, k = x.shape
    n = w.shape[2]
    tm = min(m, ROW_TILE)
    assert m % tm == 0 and n % tn == 0
    in_specs = [pl.BlockSpec((tm, k), lambda j, i: (i, 0))]
    args = [x]
    if g is not None:
        in_specs.append(pl.BlockSpec((None, 1, k), lambda j, i: (l, 0, 0)))
        args.append(g)
    in_specs.append(pl.BlockSpec((None, k, tn), lambda j, i: (l, 0, j)))
    args.append(w)
    if res is not None:
        in_specs.append(pl.BlockSpec((tm, tn), lambda j, i: (i, j)))
        args.append(res)
    return pl.pallas_call(
        functools.partial(_ws_matmul_body, norm=g is not None, residual=res is not None),
        out_shape=jax.ShapeDtypeStruct((m, n), out_dtype),
        grid=(n // tn, m // tm),
        in_specs=in_specs,
        out_specs=pl.BlockSpec((tm, tn), lambda j, i: (i, j)),
        scratch_shapes=[pltpu.VMEM((k, tn), BF16)],
        compiler_params=_params(("parallel", "arbitrary")),
        name=name,
    )(*args)


def _merge_body(oa_ref, ob_ref, oc_ref, w_ref, ga_ref, gb_ref, gc_ref, o_ref, wb_ref):
    @pl.when(pl.program_id(1) == 0)
    def _():
        wb_ref[...] = w_ref[...].astype(BF16)

    acc = None
    for b, (o_b, g_b) in enumerate(((oa_ref, ga_ref), (ob_ref, gb_ref), (oc_ref, gc_ref))):
        proj = jnp.dot(o_b[...], wb_ref[b], preferred_element_type=F32)
        term = jax.nn.sigmoid(g_b[...].astype(F32)) * proj
        acc = term if acc is None else acc + term
    o_ref[...] = acc.astype(o_ref.dtype)


def _merge(oa, ob, oc, w_branch, l, proj, *, name):
    m = oa.shape[0]
    d = w_branch.shape[3]
    tm = min(m, ROW_TILE)
    tn = 512
    g0 = COL_GATES // tn
    gstep = d // tn
    o_spec = pl.BlockSpec((tm, BRANCH_W), lambda j, i: (i, 0))

    def gate_spec(b):
        return pl.BlockSpec((tm, tn), lambda j, i: (i, g0 + b * gstep + j))

    return pl.pallas_call(
        _merge_body,
        out_shape=jax.ShapeDtypeStruct((m, d), BF16),
        grid=(d // tn, m // tm),
        in_specs=[o_spec, o_spec, o_spec,
                  pl.BlockSpec((None, 3, BRANCH_W, tn), lambda j, i: (l, 0, 0, j)),
                  gate_spec(0), gate_spec(1), gate_spec(2)],
        out_specs=pl.BlockSpec((tm, tn), lambda j, i: (i, j)),
        scratch_shapes=[pltpu.VMEM((3, BRANCH_W, tn), BF16)],
        compiler_params=_params(("parallel", "arbitrary")),
        name=name,
    )(oa, ob, oc, w_branch, proj, proj, proj)


FFN_ROWS = 512
FFN_CHUNK = 1024


def _ffn_body(h_ref, g_ref, wu_ref, wd_ref, o_ref, u_ref):
    c = pl.program_id(1)

    @pl.when(c == 0)
    def _():
        u_ref[...] = _rms(h_ref[...], g_ref[...]).astype(BF16)

    f = jnp.maximum(jnp.dot(u_ref[...], wu_ref[...], preferred_element_type=F32), 0.0)
    part = jnp.dot((f * f).astype(BF16), wd_ref[...], preferred_element_type=F32)

    @pl.when(c == 0)
    def _():
        o_ref[...] = h_ref[...] + part

    @pl.when(c > 0)
    def _():
        o_ref[...] += part


def _ffn(h, g, w_up_b, w_down_b, l, *, name):
    m, d = h.shape
    dff = w_up_b.shape[2]
    tm = min(m, FFN_ROWS)
    tc = FFN_CHUNK
    return pl.pallas_call(
        _ffn_body,
        out_shape=jax.ShapeDtypeStruct((m, d), F32),
        grid=(m // tm, dff // tc),
        in_specs=[pl.BlockSpec((tm, d), lambda i, c: (i, 0)),
                  pl.BlockSpec((None, 1, d), lambda i, c: (l, 0, 0)),
                  pl.BlockSpec((None, d, tc), lambda i, c: (l, 0, c)),
                  pl.BlockSpec((None, tc, d), lambda i, c: (l, c, 0))],
        out_specs=pl.BlockSpec((tm, d), lambda i, c: (i, 0)),
        scratch_shapes=[pltpu.VMEM((tm, d), BF16)],
        compiler_params=_params(("parallel", "arbitrary")),
        name=name,
    )(h, g, w_up_b, w_down_b)


def _rmsnorm_body(x_ref, g_ref, o_ref):
    o_ref[...] = _rms(x_ref[...], g_ref[...])


def _rmsnorm(x, g, *, name):
    m, k = x.shape
    tm = min(m, 512)
    return pl.pallas_call(
        _rmsnorm_body,
        out_shape=jax.ShapeDtypeStruct((m, k), F32),
        grid=(m // tm,),
        in_specs=[pl.BlockSpec((tm, k), lambda i: (i, 0)), pl.BlockSpec((1, k), lambda i: (0, 0))],
        out_specs=pl.BlockSpec((tm, k), lambda i: (i, 0)),
        compiler_params=_params(("parallel",)),
        name=name,
    )(x, g.reshape(1, k).astype(F32))


def _sink_softmax_pv(s, mask, sink, v, v_contract):
    s = jnp.where(mask, s * (A_HEAD_DIM ** -0.5), NEG_INF)
    m = jnp.maximum(jnp.max(s, axis=-1, keepdims=True), sink)
    p = jnp.exp(s - m)
    denom = jnp.sum(p, axis=-1, keepdims=True) + jnp.exp(sink - m)
    o = lax.dot_general(p.astype(BF16), v, (((1,), (v_contract,)), ((), ())), preferred_element_type=F32)
    return o / denom


def _swa_prompt_body(sink_ref, q_ref, kp_ref, kc_ref, vp_ref, vc_ref, o_ref, *, layer):
    n = pl.program_id(1)
    q = q_ref[...]
    k = jnp.concatenate([kp_ref[...], kc_ref[...]], axis=0)
    v = jnp.concatenate([vp_ref[...], vc_ref[...]], axis=0)
    row = lax.broadcasted_iota(jnp.int32, (WINDOW, 2 * WINDOW), 0)
    rel = lax.broadcasted_iota(jnp.int32, (WINDOW, 2 * WINDOW), 1) - WINDOW
    first_key = jnp.where(n > 0, -WINDOW, 0)
    mask = (rel <= row) & (rel > row - WINDOW) & (rel >= first_key)
    for h in range(A_KV_HEADS):
        kh = k[:, h * A_HEAD_DIM:(h + 1) * A_HEAD_DIM]
        vh = v[:, h * A_HEAD_DIM:(h + 1) * A_HEAD_DIM]
        outs = []
        for g in range(A_GROUP):
            hh = h * A_GROUP + g
            qh = q[:, hh * A_HEAD_DIM:(hh + 1) * A_HEAD_DIM]
            s = lax.dot_general(qh, kh, (((1,), (1,)), ((), ())), preferred_element_type=F32)
            outs.append(_sink_softmax_pv(s, mask, sink_ref[layer, hh], vh, 0))
        o_ref[:, h * A_KV_W:(h + 1) * A_KV_W] = jnp.concatenate(outs, axis=1).astype(o_ref.dtype)


def _swa_prompt(proj, sink, l, batch, seq, *, name):
    nb = seq // WINDOW
    ka, va = COL_KA // A_KV_W, COL_VA // A_KV_W
    kv = (WINDOW, A_KV_W)
    return pl.pallas_call(
        functools.partial(_swa_prompt_body, layer=l),
        out_shape=jax.ShapeDtypeStruct((batch * seq, BRANCH_W), BF16),
        grid=(batch, nb),
        in_specs=[_smem_spec(),
                  pl.BlockSpec((WINDOW, BRANCH_W), lambda b, n: (b * nb + n, COL_QA // BRANCH_W)),
                  pl.BlockSpec(kv, lambda b, n: (b * nb + jnp.maximum(n - 1, 0), ka)),
                  pl.BlockSpec(kv, lambda b, n: (b * nb + n, ka)),
                  pl.BlockSpec(kv, lambda b, n: (b * nb + jnp.maximum(n - 1, 0), va)),
                  pl.BlockSpec(kv, lambda b, n: (b * nb + n, va))],
        out_specs=pl.BlockSpec((WINDOW, BRANCH_W), lambda b, n: (b * nb + n, 0)),
        compiler_params=_params(("parallel", "parallel")),
        name=name,
    )(sink, proj, proj, proj, proj, proj)


SAMPLE_T = 8
SWA_SAMPLE_G = 8


def _swa_sample_body(*refs, layer, aliased):
    sink_ref, q_ref, kn_ref, vn_ref, ck_ref, cv_ref = refs[:6]
    o_ref, cko_ref, cvo_ref, o_sc = refs[-4:]
    qf = q_ref[...].astype(F32)
    knf = kn_ref[...].astype(F32)
    vnf = vn_ref[...].astype(F32)
    rows = A_GROUP * SAMPLE_T
    t_idx = lax.broadcasted_iota(jnp.int32, (rows, 2 * WINDOW), 0) % SAMPLE_T
    rel = lax.broadcasted_iota(jnp.int32, (rows, 2 * WINDOW), 1) - WINDOW
    mask = (rel <= t_idx) & (rel > t_idx - WINDOW) & (rel < SAMPLE_T)
    grp = lax.broadcasted_iota(jnp.int32, (rows, 1), 0) // SAMPLE_T
    lane = lax.broadcasted_iota(jnp.int32, (A_HEAD_DIM, WINDOW), 1)
    pad = jnp.zeros((WINDOW - SAMPLE_T, A_KV_W), F32)
    for g in range(SWA_SAMPLE_G):
        r0 = g * SAMPLE_T
        knt = jnp.concatenate([knf[r0:r0 + SAMPLE_T], pad], axis=0).T
        vnt = jnp.concatenate([vnf[r0:r0 + SAMPLE_T], pad], axis=0).T
        q_g = qf[r0:r0 + SAMPLE_T]
        for h in range(A_KV_HEADS):
            hs = slice(h * A_HEAD_DIM, (h + 1) * A_HEAD_DIM)
            ck, cv = ck_ref[g, h], cv_ref[g, h]
            kn_h, vn_h = knt[hs], vnt[hs]
            keep = lane < WINDOW - SAMPLE_T
            cko_ref[g, h] = jnp.where(keep, pltpu.roll(ck, WINDOW - SAMPLE_T, 1), pltpu.roll(kn_h, WINDOW - SAMPLE_T, 1))
            cvo_ref[g, h] = jnp.where(keep, pltpu.roll(cv, WINDOW - SAMPLE_T, 1), pltpu.roll(vn_h, WINDOW - SAMPLE_T, 1))
            kk = jnp.concatenate([ck, kn_h], axis=1).astype(BF16)
            vv = jnp.concatenate([cv, vn_h], axis=1).astype(BF16)
            q4 = jnp.concatenate(
                [q_g[:, (h * A_GROUP + gg) * A_HEAD_DIM:(h * A_GROUP + gg + 1) * A_HEAD_DIM]
                 for gg in range(A_GROUP)], axis=0).astype(BF16)
            sink = jnp.zeros((rows, 1), F32)
            for gg in range(A_GROUP):
                sink = jnp.where(grp == gg, sink_ref[layer, h * A_GROUP + gg], sink)
            s = jnp.dot(q4, kk, preferred_element_type=F32)
            o4 = _sink_softmax_pv(s, mask, sink, vv, 1)
            o_sc[r0:r0 + SAMPLE_T, h * A_KV_W:(h + 1) * A_KV_W] = jnp.concatenate(
                [o4[gg * SAMPLE_T:(gg + 1) * SAMPLE_T] for gg in range(A_GROUP)], axis=1)
    o_ref[...] = o_sc[...].astype(o_ref.dtype)


def _swa_sample(proj, sink, l, cache_k, cache_v, prev_k, prev_v, *, name):
    depth, nseq = cache_k.shape[:2]
    g = SWA_SAMPLE_G
    rows = g * SAMPLE_T
    ka, va = COL_KA // A_KV_W, COL_VA // A_KV_W
    cache_spec = pl.BlockSpec((None, g, A_KV_HEADS, A_HEAD_DIM, WINDOW), lambda i: (l, i, 0, 0, 0))
    cache_shape = jax.ShapeDtypeStruct(cache_k.shape, F32)
    in_specs = [_smem_spec(),
                pl.BlockSpec((rows, BRANCH_W), lambda i: (i, COL_QA // BRANCH_W)),
                pl.BlockSpec((rows, A_KV_W), lambda i: (i, ka)),
                pl.BlockSpec((rows, A_KV_W), lambda i: (i, va)),
                cache_spec, cache_spec]
    args = [sink, proj, proj, proj, cache_k, cache_v]
    aliases = {}
    if prev_k is not None:
        in_specs += [pl.BlockSpec(memory_space=pl.ANY)] * 2
        aliases = {len(args): 1, len(args) + 1: 2}
        args += [prev_k, prev_v]
    return pl.pallas_call(
        functools.partial(_swa_sample_body, layer=l, aliased=prev_k is not None),
        out_shape=(jax.ShapeDtypeStruct((nseq * SAMPLE_T, BRANCH_W), BF16), cache_shape, cache_shape),
        grid=(nseq // g,),
        in_specs=in_specs,
        out_specs=(pl.BlockSpec((rows, BRANCH_W), lambda i: (i, 0)), cache_spec, cache_spec),
        scratch_shapes=[pltpu.VMEM((rows, BRANCH_W), F32)],
        input_output_aliases=aliases,
        compiler_params=_params(("parallel",)),
        name=name,
    )(*args)


def _lru_gates(xc, wa_ref, ba_ref, wx_ref, bx_ref, lam_ref):
    xb = xc.astype(BF16)
    ngroups, gw, _ = wa_ref.shape

    def blockdiag(w_ref):
        return jnp.concatenate(
            [jnp.dot(xb[:, c * gw:(c + 1) * gw], w_ref[c], preferred_element_type=F32)
             for c in range(ngroups)], axis=1)

    r = jax.nn.sigmoid(blockdiag(wa_ref) + ba_ref[...])
    i = jax.nn.sigmoid(blockdiag(wx_ref) + bx_ref[...])
    nl = -lam_ref[...]
    softplus = jnp.maximum(nl, 0.0) + jnp.log1p(jnp.exp(-jnp.abs(nl)))
    log_a = (-LRU_C * r) * softplus
    a = jnp.exp(log_a)
    th = jnp.tanh(log_a)
    mult = jnp.sqrt((-2.0 * th) / (1.0 - th))
    return a, mult, i


LRU_ROWS = 256


def _lru_prompt_body(x_ref, y_ref, cw_ref, cb_ref, wa_ref, ba_ref, wx_ref, bx_ref, lam_ref,
                     o_ref, hlast_ref, xs_ref, a_ref, u_ref, h_ref):
    t = pl.program_id(1)
    rows, width = x_ref.shape

    @pl.when(t == 0)
    def _():
        xs_ref[0:SUBLANES, :] = jnp.zeros((SUBLANES, width), F32)
        h_ref[...] = jnp.zeros((SUBLANES, width), F32)

    x = x_ref[...].astype(F32)
    xs_ref[SUBLANES:SUBLANES + rows, :] = x
    cw = cw_ref[...]
    taps = [xs_ref[SUBLANES - 3 + j:SUBLANES - 3 + j + rows, :] * cw[j:j + 1] for j in range(CONV_WIDTH - 1)]
    taps.append(x * cw[CONV_WIDTH - 1:CONV_WIDTH])
    xc = cb_ref[...] + (((taps[0] + taps[1]) + taps[2]) + taps[3])
    xs_ref[0:SUBLANES, :] = xs_ref[rows:rows + SUBLANES, :]

    a, mult, i = _lru_gates(xc, wa_ref, ba_ref, wx_ref, bx_ref, lam_ref)
    first = (lax.broadcasted_iota(jnp.int32, (rows, 1), 0) == 0) & (t == 0)
    mult = jnp.where(first, 1.0, mult)
    a_ref[...] = a
    u_ref[...] = mult * (i * xc)

    rid = lax.broadcasted_iota(jnp.int32, (SUBLANES, width), 0)

    def tile(j, h):
        off = pl.multiple_of(j * SUBLANES, SUBLANES)
        at = a_ref[pl.ds(off, SUBLANES), :]
        ut = u_ref[pl.ds(off, SUBLANES), :]
        for d in (1, 2, 4):
            keep = rid >= d
            a_sh = jnp.where(keep, pltpu.roll(at, d, 0), 1.0)
            u_sh = jnp.where(keep, pltpu.roll(ut, d, 0), 0.0)
            ut = at * u_sh + ut
            at = at * a_sh
        hh = at * h + ut
        u_ref[pl.ds(off, SUBLANES), :] = hh
        return jnp.broadcast_to(hh[SUBLANES - 1:SUBLANES, :], (SUBLANES, width))

    h = lax.fori_loop(0, rows // SUBLANES, tile, h_ref[...], unroll=2)
    h_ref[...] = h
    hlast_ref[0] = h[0:1, :]
    o_ref[...] = (jax.nn.gelu(y_ref[...].astype(F32)) * u_ref[...]).astype(o_ref.dtype)


def _lru_prompt(proj, l, batch, seq, conv_w, conv_b, wa, ba, wx, bx, lam, *, name):
    rows = min(LRU_ROWS, seq)
    nt = seq // rows
    w = BRANCH_W
    vec = pl.BlockSpec((None, 1, w), lambda b, t: (l, 0, 0))
    gate_w = pl.BlockSpec((None,) + wa.shape[1:], lambda b, t: (l, 0, 0, 0))
    return pl.pallas_call(
        _lru_prompt_body,
        out_shape=(jax.ShapeDtypeStruct((batch * seq, w), BF16), jax.ShapeDtypeStruct((batch, 1, w), F32)),
        grid=(batch, nt),
        in_specs=[pl.BlockSpec((rows, w), lambda b, t: (b * nt + t, COL_XR // w)),
                  pl.BlockSpec((rows, w), lambda b, t: (b * nt + t, COL_YR // w)),
                  pl.BlockSpec((None, CONV_WIDTH, w), lambda b, t: (l, 0, 0)),
                  vec, gate_w, vec, gate_w, vec, vec],
        out_specs=(pl.BlockSpec((rows, w), lambda b, t: (b * nt + t, 0)),
                   pl.BlockSpec((1, 1, w), lambda b, t: (b, 0, 0))),
        scratch_shapes=[pltpu.VMEM((rows + SUBLANES, w), F32), pltpu.VMEM((rows, w), F32),
                        pltpu.VMEM((rows, w), F32), pltpu.VMEM((SUBLANES, w), F32)],
        compiler_params=_params(("parallel", "arbitrary")),
        name=name,
    )(proj, proj, conv_w, conv_b, wa, ba, wx, bx, lam)


def _lru_sample_body(x_ref, y_ref, c_ref, h0_ref, cw_ref, cb_ref, wa_ref, ba_ref,
                     wx_ref, bx_ref, lam_ref, o_ref, co_ref, ho_ref, xs_ref, ys_ref, os_ref):
    nseq = h0_ref.shape[0]
    xs_ref[...] = x_ref[...].astype(F32)
    ys_ref[...] = y_ref[...].astype(F32)

    def step(ref, t):
        return ref[pl.ds(t, nseq, stride=SAMPLE_T), :]

    hist = [c_ref[j] for j in range(CONV_WIDTH - 1)] + [step(xs_ref, t) for t in range(SAMPLE_T)]
    cw = cw_ref[...]
    xcs = []
    for t in range(SAMPLE_T):
        taps = [hist[t + j] * cw[j:j + 1] for j in range(CONV_WIDTH)]
        xcs.append(cb_ref[...] + (((taps[0] + taps[1]) + taps[2]) + taps[3]))
    xc = jnp.concatenate(xcs, axis=0)
    a, mult, i = _lru_gates(xc, wa_ref, ba_ref, wx_ref, bx_ref, lam_ref)
    u = mult * (i * xc)
    h = h0_ref[...]
    for t in range(SAMPLE_T):
        h = a[t * nseq:(t + 1) * nseq] * h + u[t * nseq:(t + 1) * nseq]
        os_ref[pl.ds(t, nseq, stride=SAMPLE_T), :] = jax.nn.gelu(step(ys_ref, t)) * h
    o_ref[...] = os_ref[...].astype(o_ref.dtype)
    ho_ref[...] = h
    for j in range(CONV_WIDTH - 1):
        co_ref[j] = hist[SAMPLE_T + j]


def _lru_sample(proj, l, state_conv_t, state_lru, conv_w, conv_b, wa, ba, wx, bx, lam, *, name):
    _, nseq, w = state_lru.shape
    rows = nseq * SAMPLE_T
    cw = LANES
    nc = w // cw
    assert wa.shape[1:] == (nc, cw, cw)
    vec = pl.BlockSpec((None, 1, cw), lambda c: (l, 0, c))
    gate_w = pl.BlockSpec((None, 1, cw, cw), lambda c: (l, c, 0, 0))
    ob, conv_new, h = pl.pallas_call(
        _lru_sample_body,
        out_shape=(jax.ShapeDtypeStruct((rows, w), BF16),
                   jax.ShapeDtypeStruct((CONV_WIDTH - 1, nseq, w), F32),
                   jax.ShapeDtypeStruct((nseq, w), F32)),
        grid=(nc,),
        in_specs=[pl.BlockSpec((rows, cw), lambda c: (0, COL_XR // cw + c)),
                  pl.BlockSpec((rows, cw), lambda c: (0, COL_YR // cw + c)),
                  pl.BlockSpec((None, CONV_WIDTH - 1, nseq, cw), lambda c: (l, 0, 0, c)),
                  pl.BlockSpec((None, nseq, cw), lambda c: (l, 0, c)),
                  pl.BlockSpec((None, CONV_WIDTH, cw), lambda c: (l, 0, c)),
                  vec, gate_w, vec, gate_w, vec, vec],
        out_specs=(pl.BlockSpec((rows, cw), lambda c: (0, c)),
                   pl.BlockSpec((CONV_WIDTH - 1, nseq, cw), lambda c: (0, 0, c)),
                   pl.BlockSpec((nseq, cw), lambda c: (0, c))),
        scratch_shapes=[pltpu.VMEM((rows, cw), F32), pltpu.VMEM((rows, cw), F32), pltpu.VMEM((rows, cw), F32)],
        compiler_params=_params(("parallel",)),
        name=name,
    )(proj, proj, state_conv_t, state_lru, conv_w, conv_b, wa, ba, wx, bx, lam)
    return ob, conv_new, h


def _rotate_pairs(x, cos, sin_signed, even_lane):
    width = x.shape[1]
    partner = jnp.where(even_lane, pltpu.roll(x, width - 1, 1), pltpu.roll(x, 1, 1))
    return x * cos + partner * sin_signed


def _retention_head(qh, kh, vh, s_prev, dmat, qdec, kdec, cdec):
    inner = lax.dot_general(qh.astype(BF16), kh.astype(BF16), (((1,), (1,)), ((), ())),
                            preferred_element_type=F32) * dmat
    o = jnp.dot(inner.astype(BF16), vh, preferred_element_type=F32)
    o = o + jnp.dot((qh * qdec).astype(BF16), s_prev.astype(BF16), preferred_element_type=F32)
    s_new = cdec * s_prev + lax.dot_general((kh * kdec).astype(BF16), vh, (((0,), (0,)), ((), ())),
                                            preferred_element_type=F32)
    return o, s_new


def _group_norm_gate(o, gate, gain):
    mu = jnp.mean(o, axis=-1, keepdims=True)
    oc = o - mu
    var = jnp.mean(oc * oc, axis=-1, keepdims=True)
    return jax.nn.silu(gate) * (oc * lax.rsqrt(var + EPS) * gain)


def _ret_prompt_body(cdec_ref, q_ref, k_ref, v_ref, g_ref, cos_ref, sin_ref, dmat_ref, qdec_ref, kdec_ref,
                     gn_ref, o_ref, s_ref):
    @pl.when(pl.program_id(1) == 0)
    def _():
        s_ref[...] = jnp.zeros(s_ref.shape, F32)

    rows, width = q_ref.shape
    even = (lax.broadcasted_iota(jnp.int32, (rows, width), 1) & 1) == 0
    cos = jnp.concatenate([cos_ref[...]] * R_HEADS, axis=1)
    sin = jnp.concatenate([sin_ref[...]] * R_HEADS, axis=1)
    qr = _rotate_pairs(q_ref[...].astype(F32), cos, sin, even)
    kr = _rotate_pairs(k_ref[...].astype(F32), cos, sin, even) * (R_HEAD_DIM ** -0.5)
    v = v_ref[...]
    gate = g_ref[...].astype(F32)
    qdec = qdec_ref[...]
    kdec = kdec_ref[...]
    gn = gn_ref[...]
    for h in range(R_HEADS):
        sl = slice(h * R_HEAD_DIM, (h + 1) * R_HEAD_DIM)
        o, s_new = _retention_head(qr[:, sl], kr[:, sl], v[:, sl], s_ref[0, h], dmat_ref[h],
                                   qdec[:, sl], kdec[:, sl], cdec_ref[h])
        s_ref[0, h] = s_new
        o_ref[:, sl] = _group_norm_gate(o, gate[:, sl], gn[:, sl]).astype(o_ref.dtype)


def _ret_tables(pos, chunk):
    half = R_HEAD_DIM // 2
    inv = 1.0 / (RET_THETA ** jnp.linspace(0.0, 1.0, half, dtype=F32))
    ang = pos.astype(F32)[:, None] * inv[None, :]
    cos = jnp.repeat(jnp.cos(ang), 2, axis=1)
    sin = jnp.sin(ang)
    sin_signed = jnp.stack([-sin, sin], axis=-1).reshape(pos.shape[0], R_HEAD_DIM)
    lg = jnp.log1p(-jnp.exp2(-5.0 - jnp.arange(R_HEADS, dtype=F32)))
    n = jnp.arange(chunk, dtype=F32)
    diff = n[:, None] - n[None, :]
    dmat = jnp.where(diff >= 0, jnp.exp(jnp.maximum(diff, 0.0)[None] * lg[:, None, None]), 0.0)
    qdec = jnp.repeat(jnp.exp((n[:, None] + 1.0) * lg[None, :]), R_HEAD_DIM, axis=1)
    kdec = jnp.repeat(jnp.exp((chunk - 1.0 - n)[:, None] * lg[None, :]), R_HEAD_DIM, axis=1)
    cdec = jnp.exp(chunk * lg)
    return cos, sin_signed, dmat, qdec, kdec, cdec


def _ret_prompt(proj, l, batch, seq, ret_gn, tables, *, name):
    c = RET_CHUNK
    nc = seq // c
    w = BRANCH_W
    cos, sin, dmat, qdec, kdec, cdec = tables

    def col(off):
        return pl.BlockSpec((c, w), lambda b, n: (b * nc + n, off // w))

    const2 = lambda b, n: (0, 0)
    return pl.pallas_call(
        _ret_prompt_body,
        out_shape=(jax.ShapeDtypeStruct((batch * seq, w), BF16),
                   jax.ShapeDtypeStruct((batch, R_HEADS, R_HEAD_DIM, R_HEAD_DIM), F32)),
        grid=(batch, nc),
        in_specs=[_smem_spec(), col(COL_QC), col(COL_KC), col(COL_VC), col(COL_GC),
                  pl.BlockSpec((c, R_HEAD_DIM), lambda b, n: (n, 0)),
                  pl.BlockSpec((c, R_HEAD_DIM), lambda b, n: (n, 0)),
                  pl.BlockSpec((R_HEADS, c, c), lambda b, n: (0, 0, 0)),
                  pl.BlockSpec((c, w), const2), pl.BlockSpec((c, w), const2),
                  pl.BlockSpec((None, 1, w), lambda b, n: (l, 0, 0))],
        out_specs=(pl.BlockSpec((c, w), lambda b, n: (b * nc + n, 0)),
                   pl.BlockSpec((1, R_HEADS, R_HEAD_DIM, R_HEAD_DIM), lambda b, n: (b, 0, 0, 0))),
        compiler_params=_params(("parallel", "arbitrary")),
        name=name,
    )(cdec, proj, proj, proj, proj, cos, sin, dmat, qdec, kdec, ret_gn)


RET_SAMPLE_G = 4


def _ret_sample_body(*refs):
    (cdec_ref, q_ref, k_ref, v_ref, g_ref, cos_ref, sin_ref, dmat_ref, qdec_ref, kdec_ref,
     gn_ref, s_ref) = refs[:12]
    o_ref, so_ref, o_sc = refs[-3:]
    rows, width = q_ref.shape
    even = (lax.broadcasted_iota(jnp.int32, (rows, width), 1) & 1) == 0
    cos = cos_ref[...]
    sin = sin_ref[...]
    qr = _rotate_pairs(q_ref[...].astype(F32), cos, sin, even)
    kr = _rotate_pairs(k_ref[...].astype(F32), cos, sin, even) * (R_HEAD_DIM ** -0.5)
    vf = v_ref[...].astype(F32)
    gate = g_ref[...].astype(F32)
    qdec = qdec_ref[...]
    kdec = kdec_ref[...]
    gn = gn_ref[...]
    for g in range(RET_SAMPLE_G):
        rs = slice(g * SAMPLE_T, (g + 1) * SAMPLE_T)
        for h in range(R_HEADS):
            sl = slice(h * R_HEAD_DIM, (h + 1) * R_HEAD_DIM)
            o, s_new = _retention_head(qr[rs, sl], kr[rs, sl], vf[rs, sl].astype(BF16), s_ref[g, h],
                                       dmat_ref[h], qdec[rs, sl], kdec[rs, sl], cdec_ref[h])
            so_ref[g, h] = s_new
            o_sc[rs, sl] = _group_norm_gate(o, gate[rs, sl], gn[:, sl])
    o_ref[...] = o_sc[...].astype(o_ref.dtype)


def _ret_sample(proj, l, state, prev, ret_gn, tables, *, name):
    nseq = state.shape[1]
    g = RET_SAMPLE_G
    rows = g * SAMPLE_T
    w = BRANCH_W
    cos, sin, dmat, qdec, kdec, cdec = tables

    def col(off):
        return pl.BlockSpec((rows, w), lambda i: (i, off // w))

    const2 = lambda i: (0, 0)
    tab = pl.BlockSpec((rows, w), const2)
    state_spec = pl.BlockSpec((None, g, R_HEADS, R_HEAD_DIM, R_HEAD_DIM), lambda i: (l, i, 0, 0, 0))
    in_specs = [_smem_spec(), col(COL_QC), col(COL_KC), col(COL_VC), col(COL_GC), tab, tab,
                pl.BlockSpec((R_HEADS, SAMPLE_T, SAMPLE_T), lambda i: (0, 0, 0)),
                tab, tab, pl.BlockSpec((None, 1, w), lambda i: (l, 0, 0)), state_spec]
    args = [cdec, proj, proj, proj, proj, cos, sin, dmat, qdec, kdec, ret_gn, state]
    aliases = {}
    if prev is not None:
        in_specs.append(pl.BlockSpec(memory_space=pl.ANY))
        aliases = {len(args): 1}
        args.append(prev)
    return pl.pallas_call(
        _ret_sample_body,
        out_shape=(jax.ShapeDtypeStruct((nseq * SAMPLE_T, w), BF16), jax.ShapeDtypeStruct(state.shape, F32)),
        grid=(nseq // g,),
        in_specs=in_specs,
        out_specs=(pl.BlockSpec((rows, w), lambda i: (i, 0)), state_spec),
        scratch_shapes=[pltpu.VMEM((rows, w), F32)],
        input_output_aliases=aliases,
        compiler_params=_params(("parallel",)),
        name=name,
    )(*args)


def _ret_sample_tables(g):
    pos = PAST_LEN + jnp.arange(SAMPLE_T)
    cos, sin, dmat, qdec, kdec, cdec = _ret_tables(pos, SAMPLE_T)
    tile_rows = lambda x: jnp.tile(x, (g, 1))
    return (tile_rows(jnp.tile(cos, (1, R_HEADS))), tile_rows(jnp.tile(sin, (1, R_HEADS))), dmat,
            tile_rows(qdec), tile_rows(kdec), cdec)


def _cross_prompt_body(q_ref, k_ref, v_ref, o_ref):
    q = q_ref[...]
    for h in range(X_HEADS):
        sl = slice(h * X_HEAD_DIM, (h + 1) * X_HEAD_DIM)
        k = k_ref[0, :, sl].astype(BF16)
        v = v_ref[0, :, sl].astype(BF16)
        s = lax.dot_general(q[:, sl], k, (((1,), (1,)), ((), ())), preferred_element_type=F32) * (X_HEAD_DIM ** -0.5)
        p = jnp.exp(s - jnp.max(s, axis=-1, keepdims=True))
        denom = jnp.sum(p, axis=-1, keepdims=True)
        o_ref[:, sl] = (jnp.dot(p.astype(BF16), v, preferred_element_type=F32) / denom).astype(o_ref.dtype)


def _cross_prompt(xq, mem_k, mem_v, batch, seq, *, name):
    tq = min(seq, 512)
    nq = seq // tq
    mem = mem_k.shape[1]
    kv = pl.BlockSpec((1, mem, X_W), lambda b, i: (b, 0, 0))
    return pl.pallas_call(
        _cross_prompt_body,
        out_shape=jax.ShapeDtypeStruct((batch * seq, X_W), BF16),
        grid=(batch, nq),
        in_specs=[pl.BlockSpec((tq, X_W), lambda b, i: (b * nq + i, 0)), kv, kv],
        out_specs=pl.BlockSpec((tq, X_W), lambda b, i: (b * nq + i, 0)),
        compiler_params=_params(("parallel", "parallel")),
        name=name,
    )(xq, mem_k, mem_v)


CROSS_SAMPLE_G = 8


def _cross_sample_body(q_ref, k_ref, v_ref, o_ref, o_sc):
    qf = q_ref[...].astype(F32)
    rows = X_HEADS * SAMPLE_T
    ncol = k_ref.shape[1]
    row_head = lax.broadcasted_iota(jnp.int32, (rows, ncol), 0) // SAMPLE_T
    col_head = lax.broadcasted_iota(jnp.int32, (rows, ncol), 1) % X_HEADS
    mask = row_head == col_head
    for g in range(CROSS_SAMPLE_G):
        q_g = qf[g * SAMPLE_T:(g + 1) * SAMPLE_T]
        q4 = jnp.concatenate([q_g[:, h * X_HEAD_DIM:(h + 1) * X_HEAD_DIM] for h in range(X_HEADS)],
                             axis=0).astype(BF16)
        s = lax.dot_general(q4, k_ref[g].astype(BF16), (((1,), (1,)), ((), ())), preferred_element_type=F32)
        s = jnp.where(mask, s * (X_HEAD_DIM ** -0.5), NEG_INF)
        p = jnp.exp(s - jnp.max(s, axis=-1, keepdims=True))
        denom = jnp.sum(p, axis=-1, keepdims=True)
        o4 = jnp.dot(p.astype(BF16), v_ref[g].astype(BF16), preferred_element_type=F32) / denom
        o_sc[g * SAMPLE_T:(g + 1) * SAMPLE_T, :] = jnp.concatenate(
            [o4[h * SAMPLE_T:(h + 1) * SAMPLE_T] for h in range(X_HEADS)], axis=1)
    o_ref[...] = o_sc[...].astype(o_ref.dtype)


def _cross_sample(xq, mem_k, mem_v, l, *, name):
    _, nseq, mh, hd = mem_k.shape
    g = CROSS_SAMPLE_G
    rows = g * SAMPLE_T
    kv = pl.BlockSpec((None, g, mh, hd), lambda i: (l, i, 0, 0))
    return pl.pallas_call(
        _cross_sample_body,
        out_shape=jax.ShapeDtypeStruct((nseq * SAMPLE_T, X_W), BF16),
        grid=(nseq // g,),
        in_specs=[pl.BlockSpec((rows, X_W), lambda i: (i, 0)), kv, kv],
        out_specs=pl.BlockSpec((rows, X_W), lambda i: (i, 0)),
        scratch_shapes=[pltpu.VMEM((rows, X_W), F32)],
        compiler_params=_params(("parallel",)),
        name=name,
    )(xq, mem_k, mem_v)


def _block_diag_groups(w, group_w):
    depth, nb, bs, _ = w.shape
    per = group_w // bs
    w = w.reshape(depth, nb // per, per, bs, bs)
    eye = jnp.eye(per, dtype=w.dtype)
    return jnp.einsum('lcipq,ij->lcipjq', w, eye).reshape(depth, nb // per, group_w, group_w).astype(BF16)


def kernel(x_prompt, x_sample, mem_prompt, cache_win_k, cache_win_v, state_conv, state_lru, state_ret, cache_mem_k, cache_mem_v, norm_mix, w_in, attn_sink, conv_w, conv_b, lru_wa, lru_ba, lru_wx, lru_bx, lru_lambda, ret_gn, w_branch, w_out, norm_cross, w_xq, w_xk, w_xv, w_xo, norm_ffn, w_up, w_down, norm_final):
    bp, tp, d = x_prompt.shape
    bs, ts, _ = x_sample.shape
    depth = w_in.shape[0]
    mlen = mem_prompt.shape[1]
    wb = cache_win_k.shape[2]
    assert ts == SAMPLE_T and wb == WINDOW and w_in.shape[2] == IN_W
    assert tp % WINDOW == 0 and tp % RET_CHUNK == 0

    w_up_b, w_down_b = w_up.astype(BF16), w_down.astype(BF16)
    wa_p, wx_p = _block_diag_groups(lru_wa, LRU_GROUP_W), _block_diag_groups(lru_wx, LRU_GROUP_W)
    wa_s, wx_s = _block_diag_groups(lru_wa, LANES), _block_diag_groups(lru_wx, LANES)
    norm_mix3, norm_cross3, norm_ffn3 = map(_layer_vec, (norm_mix, norm_cross, norm_ffn))
    conv_b3, ba3, bx3, lam3 = map(_layer_vec, (conv_b, lru_ba, lru_bx, lru_lambda))
    ret_gn3 = ret_gn.reshape(depth, 1, BRANCH_W).astype(F32)
    sink = attn_sink.astype(F32)
    win_k_t = cache_win_k.transpose(0, 1, 3, 4, 2)
    win_v_t = cache_win_v.transpose(0, 1, 3, 4, 2)
    conv_t = state_conv.transpose(0, 2, 1, 3)
    mem_k4 = cache_mem_k.reshape(depth, bs, mlen * X_HEADS, X_HEAD_DIM)
    mem_v4 = cache_mem_v.reshape(depth, bs, mlen * X_HEADS, X_HEAD_DIM)
    tables_p = _ret_tables(jnp.arange(tp), RET_CHUNK)
    tables_s = _ret_sample_tables(RET_SAMPLE_G)

    hp = x_prompt.reshape(bp * tp, d)
    hs = x_sample.reshape(bs * ts, d)
    mem2d = mem_prompt.reshape(bp * mlen, d)
    outs = {k: [] for k in ('p_wk', 'p_wv', 'p_conv', 'p_lru', 'p_ret', 'p_mk', 'p_mv', 's_conv', 's_lru')}
    s_wk = s_wv = s_ret = None
    for l in range(depth):
        lru_p = (conv_w, conv_b3, wa_p, ba3, wx_p, bx3, lam3)
        lru_s = (conv_w, conv_b3, wa_s, ba3, wx_s, bx3, lam3)

        def dense_tail(h, merged, xattn, tag):
            h = _ws_matmul(merged, w_out, l, tn=512, out_dtype=F32, res=h, name=f'out_proj_{tag}')
            xq = _ws_matmul(h, w_xq, l, tn=X_W, out_dtype=BF16, g=norm_cross3, name=f'xq_{tag}')
            h = _ws_matmul(xattn(xq), w_xo, l, tn=1024, out_dtype=F32, res=h, name=f'xo_{tag}')
            return _ffn(h, norm_ffn3, w_up_b, w_down_b, l, name=f'ffn_{tag}')

        mk = _ws_matmul(mem2d, w_xk, l, tn=X_W, out_dtype=F32, name=f'mem_k_{l}')
        mv = _ws_matmul(mem2d, w_xv, l, tn=X_W, out_dtype=F32, name=f'mem_v_{l}')
        mk3, mv3 = mk.reshape(bp, mlen, X_W), mv.reshape(bp, mlen, X_W)
        u = _rms_cast(hp, norm_mix3, l, out_dtype=BF16, name=f'norm_mix_p{l}')
        proj = _in_proj(u, w_in, l, name=f'in_proj_p{l}')
        oa = _swa_prompt(proj, sink, l, bp, tp, name=f'swa_p{l}')
        ob, lru_last = _lru_prompt(proj, l, bp, tp, *lru_p, name=f'lru_p{l}')
        oc, ret_last = _ret_prompt(proj, l, bp, tp, ret_gn3, tables_p, name=f'ret_p{l}')
        merged = _merge(oa, ob, oc, w_branch, l, proj, name=f'merge_p{l}')
        hp = dense_tail(hp, merged, lambda xq: _cross_prompt(xq, mk3, mv3, bp, tp, name=f'cross_p{l}'), f'p{l}')
        proj3 = proj.reshape(bp, tp, IN_W)
        outs['p_wk'].append(proj3[:, tp - wb:, COL_KA:COL_KA + A_KV_W].astype(F32).reshape(bp, wb, A_KV_HEADS, A_HEAD_DIM))
        outs['p_wv'].append(proj3[:, tp - wb:, COL_VA:COL_VA + A_KV_W].astype(F32).reshape(bp, wb, A_KV_HEADS, A_HEAD_DIM))
        outs['p_conv'].append(proj3[:, tp - (CONV_WIDTH - 1):, COL_XR:COL_XR + BRANCH_W].astype(F32))
        outs['p_lru'].append(lru_last.reshape(bp, BRANCH_W))
        outs['p_ret'].append(ret_last)
        outs['p_mk'].append(mk.reshape(bp, mlen, X_HEADS, X_HEAD_DIM))
        outs['p_mv'].append(mv.reshape(bp, mlen, X_HEADS, X_HEAD_DIM))

        u = _rms_cast(hs, norm_mix3, l, out_dtype=BF16, name=f'norm_mix_s{l}')
        proj = _in_proj(u, w_in, l, name=f'in_proj_s{l}')
        oa, s_wk, s_wv = _swa_sample(proj, sink, l, win_k_t, win_v_t, s_wk, s_wv, name=f'swa_s{l}')
        ob, conv_new, lru_new = _lru_sample(proj, l, conv_t, state_lru, *lru_s, name=f'lru_s{l}')
        oc, s_ret = _ret_sample(proj, l, state_ret, s_ret, ret_gn3, tables_s, name=f'ret_s{l}')
        merged = _merge(oa, ob, oc, w_branch, l, proj, name=f'merge_s{l}')
        hs = dense_tail(hs, merged, lambda xq: _cross_sample(xq, mem_k4, mem_v4, l, name=f'cross_s{l}'), f's{l}')
        outs['s_conv'].append(conv_new)
        outs['s_lru'].append(lru_new)

    y_prompt = _rmsnorm(hp, norm_final, name='final_norm_p').reshape(bp, tp, d)
    y_sample = _rmsnorm(hs, norm_final, name='final_norm_s').reshape(bs, ts, d)
    st = lambda k: jnp.stack(outs[k])
    return (y_prompt, y_sample, st('p_wk'), st('p_wv'), st('p_conv'), st('p_lru'), st('p_ret'),
            st('p_mk'), st('p_mv'),
            s_wk.transpose(0, 1, 4, 2, 3), s_wv.transpose(0, 1, 4, 2, 3),
            st('s_conv').transpose(0, 2, 1, 3), st('s_lru'), s_ret)
```

```python
import functools

import jax
import jax.numpy as jnp
from jax import lax
from jax.experimental import pallas as pl
from jax.experimental.pallas import tpu as pltpu

F32 = jnp.float32
BF16 = jnp.bfloat16
NEG_INF = -1e30
EPS = 1e-6

V7X_VMEM_BYTES = 64 * 1024 * 1024
VMEM_LIMIT_BYTES = V7X_VMEM_BYTES - 8 * 1024 * 1024
LANES = 128
SUBLANES = 8

BRANCH_W = 1024
A_HEAD_DIM = 64
A_HEADS = 16
A_KV_HEADS = 4
A_GROUP = 4
A_KV_W = 256
WINDOW = 128
PAST_LEN = 8192
LRU_C = 8.0
CONV_WIDTH = 4
LRU_GROUP_W = 256
R_HEADS = 4
R_HEAD_DIM = 256
RET_CHUNK = 128
RET_THETA = 10000.0
X_HEADS = 4
X_HEAD_DIM = 128
X_W = 512

COL_QA, COL_XR, COL_YR, COL_QC, COL_KC, COL_VC, COL_GC, COL_GATES, COL_KA, COL_VA = (
    0, 1024, 2048, 3072, 4096, 5120, 6144, 7168, 13312, 13568)
IN_W = 13824
PERM_BLOCK = 512
ROW_TILE = 1024


def _params(semantics):
    return pltpu.CompilerParams(dimension_semantics=semantics, vmem_limit_bytes=VMEM_LIMIT_BYTES)


def _smem_spec():
    return pl.BlockSpec(memory_space=pltpu.SMEM)


def _rms(x, g):
    y = x * lax.rsqrt(jnp.mean(x * x, axis=-1, keepdims=True) + EPS)
    return y * g


def _layer_vec(v):
    return v.reshape(v.shape[0], 1, v.shape[1]).astype(F32)


def _rms_cast_body(x_ref, g_ref, o_ref):
    o_ref[...] = _rms(x_ref[...], g_ref[...]).astype(o_ref.dtype)


def _rms_cast(x, g, l, *, out_dtype, name):
    m, k = x.shape
    tm = min(m, 512)
    return pl.pallas_call(
        _rms_cast_body,
        out_shape=jax.ShapeDtypeStruct((m, k), out_dtype),
        grid=(m // tm,),
        in_specs=[pl.BlockSpec((tm, k), lambda i: (i, 0)), pl.BlockSpec((None, 1, k), lambda i: (l, 0, 0))],
        out_specs=pl.BlockSpec((tm, k), lambda i: (i, 0)),
        compiler_params=_params(("parallel",)),
        name=name,
    )(x, g)


def _in_proj_body(u_ref, w0_ref, w1_ref, w2_ref, o_ref, wb_ref):
    @pl.when(pl.program_id(1) == 0)
    def _():
        for k, w_ref in enumerate((w0_ref, w1_ref, w2_ref)):
            wb_ref[:, k * PERM_BLOCK:(k + 1) * PERM_BLOCK] = w_ref[...].astype(BF16)

    o_ref[...] = jnp.dot(u_ref[...], wb_ref[...], preferred_element_type=F32).astype(o_ref.dtype)


def _in_proj(u, w_in, l, *, name):
    m, k = u.shape
    tm = min(m, ROW_TILE)
    per = 3
    tn = per * PERM_BLOCK
    nblk = IN_W // PERM_BLOCK

    def src_block(c):
        return jnp.where(c < 2, c, jnp.where(c < nblk - 1, c + 1, 2))

    def w_spec(kk):
        return pl.BlockSpec((None, k, PERM_BLOCK), lambda j, i: (l, 0, src_block(per * j + kk)))

    return pl.pallas_call(
        _in_proj_body,
        out_shape=jax.ShapeDtypeStruct((m, IN_W), BF16),
        grid=(IN_W // tn, m // tm),
        in_specs=[pl.BlockSpec((tm, k), lambda j, i: (i, 0)), w_spec(0), w_spec(1), w_spec(2)],
        out_specs=pl.BlockSpec((tm, tn), lambda j, i: (i, j)),
        scratch_shapes=[pltpu.VMEM((k, tn), BF16)],
        compiler_params=_params(("parallel", "arbitrary")),
        name=name,
    )(u, w_in, w_in, w_in)


def _ws_matmul_body(*refs, norm, residual):
    it = iter(refs)
    x_ref = next(it)
    g_ref = next(it) if norm else None
    w_ref = next(it)
    r_ref = next(it) if residual else None
    o_ref = next(it)
    wb_ref = next(it)

    @pl.when(pl.program_id(1) == 0)
    def _():
        wb_ref[...] = w_ref[...].astype(BF16)

    x = x_ref[...]
    if norm:
        x = _rms(x, g_ref[...])
    acc = jnp.dot(x.astype(BF16), wb_ref[...], preferred_element_type=F32)
    if residual:
        acc = r_ref[...] + acc
    o_ref[...] = acc.astype(o_ref.dtype)


def _ws_matmul(x, w, l, *, tn, out_dtype, g=None, res=None, name):
    m, k = x.shape
    n = w.shape[2]
    tm = min(m, ROW_TILE)
    assert m % tm == 0 and n % tn == 0
    in_specs = [pl.BlockSpec((tm, k), lambda j, i: (i, 0))]
    args = [x]
    if g is not None:
        in_specs.append(pl.BlockSpec((None, 1, k), lambda j, i: (l, 0, 0)))
        args.append(g)
    in_specs.append(pl.BlockSpec((None, k, tn), lambda j, i: (l, 0, j)))
    args.append(w)
    if res is not None:
        in_specs.append(pl.BlockSpec((tm, tn), lambda j, i: (i, j)))
        args.append(res)
    return pl.pallas_call(
        functools.partial(_ws_matmul_body, norm=g is not None, residual=res is not None),
        out_shape=jax.ShapeDtypeStruct((m, n), out_dtype),
        grid=(n // tn, m // tm),
        in_specs=in_specs,
        out_specs=pl.BlockSpec((tm, tn), lambda j, i: (i, j)),
        scratch_shapes=[pltpu.VMEM((k, tn), BF16)],
        compiler_params=_params(("parallel", "arbitrary")),
        name=name,
    )(*args)


MERGE_ROWS = 256


def _merge_out_body(oa_ref, ob_ref, oc_ref, wbr_ref, g0, g1, g2, g3, g4, g5, wout_ref, h_ref, o_ref):
    gates = ((g0, g1), (g2, g3), (g4, g5))
    half = wbr_ref.shape[2] // 2
    parts = []
    for n in range(2):
        sl = slice(n * half, (n + 1) * half)
        acc = None
        for b, o_b in enumerate((oa_ref, ob_ref, oc_ref)):
            proj = jnp.dot(o_b[...], wbr_ref[b, :, sl], preferred_element_type=F32)
            term = jax.nn.sigmoid(gates[b][n][...].astype(F32)) * proj
            acc = term if acc is None else acc + term
        parts.append(acc.astype(BF16))
    merged = jnp.concatenate(parts, axis=1)
    o_ref[...] = h_ref[...] + jnp.dot(merged, wout_ref[...], preferred_element_type=F32)


def _merge_out(oa, ob, oc, w_branch_b, w_out_b, l, proj, h, *, name):
    m, d = h.shape
    tm = min(m, MERGE_ROWS)
    gw = d // 2
    g0 = COL_GATES // gw
    o_spec = pl.BlockSpec((tm, BRANCH_W), lambda i: (i, 0))
    row = pl.BlockSpec((tm, d), lambda i: (i, 0))
    gate_specs = [pl.BlockSpec((tm, gw), functools.partial(lambda i, k: (i, g0 + k), k=k)) for k in range(6)]
    once = pl.Buffered(1)
    return pl.pallas_call(
        _merge_out_body,
        out_shape=jax.ShapeDtypeStruct((m, d), F32),
        grid=(m // tm,),
        in_specs=[o_spec, o_spec, o_spec,
                  pl.BlockSpec((None, 3, BRANCH_W, d), lambda i: (l, 0, 0, 0), pipeline_mode=once),
                  *gate_specs,
                  pl.BlockSpec((None, d, d), lambda i: (l, 0, 0), pipeline_mode=once),
                  row],
        out_specs=row,
        compiler_params=_params(("parallel",)),
        name=name,
    )(oa, ob, oc, w_branch_b, *([proj] * 6), w_out_b, h)


FFN_ROWS = 512
FFN_CHUNK = 1024
FFN_OUT_SPLIT = 4


def _ffn_body(*refs, final):
    if final:
        h_ref, g_ref, wu_ref, wd_ref, gn_ref, n_ref, u_ref, o_ref = refs
    else:
        h_ref, g_ref, wu_ref, wd_ref, gn_ref, o_ref, n_ref, u_ref = refs
    c = pl.program_id(1)

    @pl.when(c == 0)
    def _():
        h = h_ref[...]
        u_ref[...] = _rms(h, g_ref[...]).astype(BF16)
        o_ref[...] = h

    f = jnp.maximum(jnp.dot(u_ref[...], wu_ref[...], preferred_element_type=F32), 0.0)
    f2 = (f * f).astype(BF16)
    tn = o_ref.shape[1] // FFN_OUT_SPLIT
    for n in range(FFN_OUT_SPLIT):
        sl = slice(n * tn, (n + 1) * tn)
        o_ref[:, sl] += jnp.dot(f2, wd_ref[:, sl], preferred_element_type=F32)

    @pl.when(c == pl.num_programs(1) - 1)
    def _():
        n_ref[...] = _rms(o_ref[...], gn_ref[...]).astype(n_ref.dtype)


def _ffn(h, g, w_up_b, w_down_b, l, gn, gn_index, *, final, name):
    m, d = h.shape
    dff = w_up_b.shape[2]
    tm = min(m, FFN_ROWS)
    tc = FFN_CHUNK
    row = pl.BlockSpec((tm, d), lambda i, c: (i, 0))
    if final:
        out_shape = jax.ShapeDtypeStruct((m, d), F32)
        out_specs = row
        scratch = [pltpu.VMEM((tm, d), BF16), pltpu.VMEM((tm, d), F32)]
    else:
        out_shape = (jax.ShapeDtypeStruct((m, d), F32), jax.ShapeDtypeStruct((m, d), BF16))
        out_specs = (row, row)
        scratch = [pltpu.VMEM((tm, d), BF16)]
    return pl.pallas_call(
        functools.partial(_ffn_body, final=final),
        out_shape=out_shape,
        grid=(m // tm, dff // tc),
        in_specs=[row,
                  pl.BlockSpec((None, 1, d), lambda i, c: (l, 0, 0)),
                  pl.BlockSpec((None, d, tc), lambda i, c: (l, 0, c)),
                  pl.BlockSpec((None, tc, d), lambda i, c: (l, c, 0)),
                  pl.BlockSpec((None, 1, d), lambda i, c: (gn_index, 0, 0))],
        out_specs=out_specs,
        scratch_shapes=scratch,
        compiler_params=_params(("parallel", "arbitrary")),
        name=name,
    )(h, g, w_up_b, w_down_b, gn)


def _sink_softmax_pv(s, mask, sink, v, v_contract):
    s = jnp.where(mask, s * (A_HEAD_DIM ** -0.5), NEG_INF)
    m = jnp.maximum(jnp.max(s, axis=-1, keepdims=True), sink)
    p = jnp.exp(s - m)
    denom = jnp.sum(p, axis=-1, keepdims=True) + jnp.exp(sink - m)
    o = lax.dot_general(p.astype(BF16), v, (((1,), (v_contract,)), ((), ())), preferred_element_type=F32)
    return o / denom


def _swa_prompt_body(sink_ref, q_ref, kp_ref, kc_ref, vp_ref, vc_ref, o_ref, *, layer):
    n = pl.program_id(1)
    q = q_ref[...]
    k = jnp.concatenate([kp_ref[...], kc_ref[...]], axis=0)
    v = jnp.concatenate([vp_ref[...], vc_ref[...]], axis=0)
    row = lax.broadcasted_iota(jnp.int32, (WINDOW, 2 * WINDOW), 0)
    rel = lax.broadcasted_iota(jnp.int32, (WINDOW, 2 * WINDOW), 1) - WINDOW
    first_key = jnp.where(n > 0, -WINDOW, 0)
    mask = (rel <= row) & (rel > row - WINDOW) & (rel >= first_key)
    for h in range(A_KV_HEADS):
        kh = k[:, h * A_HEAD_DIM:(h + 1) * A_HEAD_DIM]
        vh = v[:, h * A_HEAD_DIM:(h + 1) * A_HEAD_DIM]
        outs = []
        for g in range(A_GROUP):
            hh = h * A_GROUP + g
            qh = q[:, hh * A_HEAD_DIM:(hh + 1) * A_HEAD_DIM]
            s = lax.dot_general(qh, kh, (((1,), (1,)), ((), ())), preferred_element_type=F32)
            outs.append(_sink_softmax_pv(s, mask, sink_ref[layer, hh], vh, 0))
        o_ref[:, h * A_KV_W:(h + 1) * A_KV_W] = jnp.concatenate(outs, axis=1).astype(o_ref.dtype)


def _swa_prompt(proj, sink, l, batch, seq, *, name):
    nb = seq // WINDOW
    ka, va = COL_KA // A_KV_W, COL_VA // A_KV_W
    kv = (WINDOW, A_KV_W)
    return pl.pallas_call(
        functools.partial(_swa_prompt_body, layer=l),
        out_shape=jax.ShapeDtypeStruct((batch * seq, BRANCH_W), BF16),
        grid=(batch, nb),
        in_specs=[_smem_spec(),
                  pl.BlockSpec((WINDOW, BRANCH_W), lambda b, n: (b * nb + n, COL_QA // BRANCH_W)),
                  pl.BlockSpec(kv, lambda b, n: (b * nb + jnp.maximum(n - 1, 0), ka)),
                  pl.BlockSpec(kv, lambda b, n: (b * nb + n, ka)),
                  pl.BlockSpec(kv, lambda b, n: (b * nb + jnp.maximum(n - 1, 0), va)),
                  pl.BlockSpec(kv, lambda b, n: (b * nb + n, va))],
        out_specs=pl.BlockSpec((WINDOW, BRANCH_W), lambda b, n: (b * nb + n, 0)),
        compiler_params=_params(("parallel", "parallel")),
        name=name,
    )(sink, proj, proj, proj, proj, proj)


SAMPLE_T = 8
SWA_SAMPLE_G = 8


def _swa_sample_body(*refs, layer, aliased):
    sink_ref, q_ref, kn_ref, vn_ref, ck_ref, cv_ref = refs[:6]
    o_ref, cko_ref, cvo_ref, o_sc = refs[-4:]
    qf = q_ref[...].astype(F32)
    knf = kn_ref[...].astype(F32)
    vnf = vn_ref[...].astype(F32)
    rows = A_GROUP * SAMPLE_T
    t_idx = lax.broadcasted_iota(jnp.int32, (rows, 2 * WINDOW), 0) % SAMPLE_T
    rel = lax.broadcasted_iota(jnp.int32, (rows, 2 * WINDOW), 1) - WINDOW
    mask = (rel <= t_idx) & (rel > t_idx - WINDOW) & (rel < SAMPLE_T)
    grp = lax.broadcasted_iota(jnp.int32, (rows, 1), 0) // SAMPLE_T
    lane = lax.broadcasted_iota(jnp.int32, (A_HEAD_DIM, WINDOW), 1)
    pad = jnp.zeros((WINDOW - SAMPLE_T, A_KV_W), F32)
    for g in range(SWA_SAMPLE_G):
        r0 = g * SAMPLE_T
        knt = jnp.concatenate([knf[r0:r0 + SAMPLE_T], pad], axis=0).T
        vnt = jnp.concatenate([vnf[r0:r0 + SAMPLE_T], pad], axis=0).T
        q_g = qf[r0:r0 + SAMPLE_T]
        for h in range(A_KV_HEADS):
            hs = slice(h * A_HEAD_DIM, (h + 1) * A_HEAD_DIM)
            ck, cv = ck_ref[g, h], cv_ref[g, h]
            kn_h, vn_h = knt[hs], vnt[hs]
            keep = lane < WINDOW - SAMPLE_T
            cko_ref[g, h] = jnp.where(keep, pltpu.roll(ck, WINDOW - SAMPLE_T, 1), pltpu.roll(kn_h, WINDOW - SAMPLE_T, 1))
            cvo_ref[g, h] = jnp.where(keep, pltpu.roll(cv, WINDOW - SAMPLE_T, 1), pltpu.roll(vn_h, WINDOW - SAMPLE_T, 1))
            kk = jnp.concatenate([ck, kn_h], axis=1).astype(BF16)
            vv = jnp.concatenate([cv, vn_h], axis=1).astype(BF16)
            q4 = jnp.concatenate(
                [q_g[:, (h * A_GROUP + gg) * A_HEAD_DIM:(h * A_GROUP + gg + 1) * A_HEAD_DIM]
                 for gg in range(A_GROUP)], axis=0).astype(BF16)
            sink = jnp.zeros((rows, 1), F32)
            for gg in range(A_GROUP):
                sink = jnp.where(grp == gg, sink_ref[layer, h * A_GROUP + gg], sink)
            s = jnp.dot(q4, kk, preferred_element_type=F32)
            o4 = _sink_softmax_pv(s, mask, sink, vv, 1)
            o_sc[r0:r0 + SAMPLE_T, h * A_KV_W:(h + 1) * A_KV_W] = jnp.concatenate(
                [o4[gg * SAMPLE_T:(gg + 1) * SAMPLE_T] for gg in range(A_GROUP)], axis=1)
    o_ref[...] = o_sc[...].astype(o_ref.dtype)


def _swa_sample(proj, sink, l, cache_k, cache_v, prev_k, prev_v, *, name):
    depth, nseq = cache_k.shape[:2]
    g = SWA_SAMPLE_G
    rows = g * SAMPLE_T
    ka, va = COL_KA // A_KV_W, COL_VA // A_KV_W
    cache_spec = pl.BlockSpec((None, g, A_KV_HEADS, A_HEAD_DIM, WINDOW), lambda i: (l, i, 0, 0, 0))
    cache_shape = jax.ShapeDtypeStruct(cache_k.shape, F32)
    in_specs = [_smem_spec(),
                pl.BlockSpec((rows, BRANCH_W), lambda i: (i, COL_QA // BRANCH_W)),
                pl.BlockSpec((rows, A_KV_W), lambda i: (i, ka)),
                pl.BlockSpec((rows, A_KV_W), lambda i: (i, va)),
                cache_spec, cache_spec]
    args = [sink, proj, proj, proj, cache_k, cache_v]
    aliases = {}
    if prev_k is not None:
        in_specs += [pl.BlockSpec(memory_space=pl.ANY)] * 2
        aliases = {len(args): 1, len(args) + 1: 2}
        args += [prev_k, prev_v]
    return pl.pallas_call(
        functools.partial(_swa_sample_body, layer=l, aliased=prev_k is not None),
        out_shape=(jax.ShapeDtypeStruct((nseq * SAMPLE_T, BRANCH_W), BF16), cache_shape, cache_shape),
        grid=(nseq // g,),
        in_specs=in_specs,
        out_specs=(pl.BlockSpec((rows, BRANCH_W), lambda i: (i, 0)), cache_spec, cache_spec),
        scratch_shapes=[pltpu.VMEM((rows, BRANCH_W), F32)],
        input_output_aliases=aliases,
        compiler_params=_params(("parallel",)),
        name=name,
    )(*args)


def _lru_gates(xc, wa_ref, ba_ref, wx_ref, bx_ref, lam_ref):
    xb = xc.astype(BF16)
    ngroups, gw, _ = wa_ref.shape

    def blockdiag(w_ref):
        return jnp.concatenate(
            [jnp.dot(xb[:, c * gw:(c + 1) * gw], w_ref[c], preferred_element_type=F32)
             for c in range(ngroups)], axis=1)

    r = jax.nn.sigmoid(blockdiag(wa_ref) + ba_ref[...])
    i = jax.nn.sigmoid(blockdiag(wx_ref) + bx_ref[...])
    nl = -lam_ref[...]
    softplus = jnp.maximum(nl, 0.0) + jnp.log1p(jnp.exp(-jnp.abs(nl)))
    log_a = (-LRU_C * r) * softplus
    a = jnp.exp(log_a)
    th = jnp.tanh(log_a)
    mult = jnp.sqrt((-2.0 * th) / (1.0 - th))
    return a, mult, i


LRU_ROWS = 256


def _lru_prompt_body(x_ref, y_ref, cw_ref, cb_ref, wa_ref, ba_ref, wx_ref, bx_ref, lam_ref,
                     o_ref, hlast_ref, xs_ref, a_ref, u_ref, h_ref):
    t = pl.program_id(1)
    rows, width = x_ref.shape

    @pl.when(t == 0)
    def _():
        xs_ref[0:SUBLANES, :] = jnp.zeros((SUBLANES, width), F32)
        h_ref[...] = jnp.zeros((SUBLANES, width), F32)

    x = x_ref[...].astype(F32)
    xs_ref[SUBLANES:SUBLANES + rows, :] = x
    cw = cw_ref[...]
    taps = [xs_ref[SUBLANES - 3 + j:SUBLANES - 3 + j + rows, :] * cw[j:j + 1] for j in range(CONV_WIDTH - 1)]
    taps.append(x * cw[CONV_WIDTH - 1:CONV_WIDTH])
    xc = cb_ref[...] + (((taps[0] + taps[1]) + taps[2]) + taps[3])
    xs_ref[0:SUBLANES, :] = xs_ref[rows:rows + SUBLANES, :]

    a, mult, i = _lru_gates(xc, wa_ref, ba_ref, wx_ref, bx_ref, lam_ref)
    first = (lax.broadcasted_iota(jnp.int32, (rows, 1), 0) == 0) & (t == 0)
    mult = jnp.where(first, 1.0, mult)
    a_ref[...] = a
    u_ref[...] = mult * (i * xc)

    rid = lax.broadcasted_iota(jnp.int32, (SUBLANES, width), 0)

    def tile(j, h):
        off = pl.multiple_of(j * SUBLANES, SUBLANES)
        at = a_ref[pl.ds(off, SUBLANES), :]
        ut = u_ref[pl.ds(off, SUBLANES), :]
        for d in (1, 2, 4):
            keep = rid >= d
            a_sh = jnp.where(keep, pltpu.roll(at, d, 0), 1.0)
            u_sh = jnp.where(keep, pltpu.roll(ut, d, 0), 0.0)
            ut = at * u_sh + ut
            at = at * a_sh
        hh = at * h + ut
        u_ref[pl.ds(off, SUBLANES), :] = hh
        return jnp.broadcast_to(hh[SUBLANES - 1:SUBLANES, :], (SUBLANES, width))

    h = lax.fori_loop(0, rows // SUBLANES, tile, h_ref[...], unroll=2)
    h_ref[...] = h
    hlast_ref[0] = h[0:1, :]
    o_ref[...] = (jax.nn.gelu(y_ref[...].astype(F32)) * u_ref[...]).astype(o_ref.dtype)


def _lru_prompt(proj, l, batch, seq, conv_w, conv_b, wa, ba, wx, bx, lam, *, name):
    rows = min(LRU_ROWS, seq)
    nt = seq // rows
    w = BRANCH_W
    vec = pl.BlockSpec((None, 1, w), lambda b, t: (l, 0, 0))
    gate_w = pl.BlockSpec((None,) + wa.shape[1:], lambda b, t: (l, 0, 0, 0))
    return pl.pallas_call(
        _lru_prompt_body,
        out_shape=(jax.ShapeDtypeStruct((batch * seq, w), BF16), jax.ShapeDtypeStruct((batch, 1, w), F32)),
        grid=(batch, nt),
        in_specs=[pl.BlockSpec((rows, w), lambda b, t: (b * nt + t, COL_XR // w)),
                  pl.BlockSpec((rows, w), lambda b, t: (b * nt + t, COL_YR // w)),
                  pl.BlockSpec((None, CONV_WIDTH, w), lambda b, t: (l, 0, 0)),
                  vec, gate_w, vec, gate_w, vec, vec],
        out_specs=(pl.BlockSpec((rows, w), lambda b, t: (b * nt + t, 0)),
                   pl.BlockSpec((1, 1, w), lambda b, t: (b, 0, 0))),
        scratch_shapes=[pltpu.VMEM((rows + SUBLANES, w), F32), pltpu.VMEM((rows, w), F32),
                        pltpu.VMEM((rows, w), F32), pltpu.VMEM((SUBLANES, w), F32)],
        compiler_params=_params(("parallel", "arbitrary")),
        name=name,
    )(proj, proj, conv_w, conv_b, wa, ba, wx, bx, lam)


def _lru_sample_body(x_ref, y_ref, c_ref, h0_ref, cw_ref, cb_ref, wa_ref, ba_ref,
                     wx_ref, bx_ref, lam_ref, o_ref, co_ref, ho_ref, xs_ref, ys_ref, os_ref):
    nseq = h0_ref.shape[0]
    xs_ref[...] = x_ref[...].astype(F32)
    ys_ref[...] = y_ref[...].astype(F32)

    def step(ref, t):
        return ref[pl.ds(t, nseq, stride=SAMPLE_T), :]

    hist = [c_ref[j] for j in range(CONV_WIDTH - 1)] + [step(xs_ref, t) for t in range(SAMPLE_T)]
    cw = cw_ref[...]
    xcs = []
    for t in range(SAMPLE_T):
        taps = [hist[t + j] * cw[j:j + 1] for j in range(CONV_WIDTH)]
        xcs.append(cb_ref[...] + (((taps[0] + taps[1]) + taps[2]) + taps[3]))
    xc = jnp.concatenate(xcs, axis=0)
    a, mult, i = _lru_gates(xc, wa_ref, ba_ref, wx_ref, bx_ref, lam_ref)
    u = mult * (i * xc)
    h = h0_ref[...]
    for t in range(SAMPLE_T):
        h = a[t * nseq:(t + 1) * nseq] * h + u[t * nseq:(t + 1) * nseq]
        os_ref[pl.ds(t, nseq, stride=SAMPLE_T), :] = jax.nn.gelu(step(ys_ref, t)) * h
    o_ref[...] = os_ref[...].astype(o_ref.dtype)
    ho_ref[...] = h
    for j in range(CONV_WIDTH - 1):
        co_ref[j] = hist[SAMPLE_T + j]


def _lru_sample(proj, l, state_conv_t, state_lru, conv_w, conv_b, wa, ba, wx, bx, lam, *, name):
    _, nseq, w = state_lru.shape
    rows = nseq * SAMPLE_T
    cw = LANES
    nc = w // cw
    assert wa.shape[1:] == (nc, cw, cw)
    vec = pl.BlockSpec((None, 1, cw), lambda c: (l, 0, c))
    gate_w = pl.BlockSpec((None, 1, cw, cw), lambda c: (l, c, 0, 0))
    ob, conv_new, h = pl.pallas_call(
        _lru_sample_body,
        out_shape=(jax.ShapeDtypeStruct((rows, w), BF16),
                   jax.ShapeDtypeStruct((CONV_WIDTH - 1, nseq, w), F32),
                   jax.ShapeDtypeStruct((nseq, w), F32)),
        grid=(nc,),
        in_specs=[pl.BlockSpec((rows, cw), lambda c: (0, COL_XR // cw + c)),
                  pl.BlockSpec((rows, cw), lambda c: (0, COL_YR // cw + c)),
                  pl.BlockSpec((None, CONV_WIDTH - 1, nseq, cw), lambda c: (l, 0, 0, c)),
                  pl.BlockSpec((None, nseq, cw), lambda c: (l, 0, c)),
                  pl.BlockSpec((None, CONV_WIDTH, cw), lambda c: (l, 0, c)),
                  vec, gate_w, vec, gate_w, vec, vec],
        out_specs=(pl.BlockSpec((rows, cw), lambda c: (0, c)),
                   pl.BlockSpec((CONV_WIDTH - 1, nseq, cw), lambda c: (0, 0, c)),
                   pl.BlockSpec((nseq, cw), lambda c: (0, c))),
        scratch_shapes=[pltpu.VMEM((rows, cw), F32), pltpu.VMEM((rows, cw), F32), pltpu.VMEM((rows, cw), F32)],
        compiler_params=_params(("parallel",)),
        name=name,
    )(proj, proj, state_conv_t, state_lru, conv_w, conv_b, wa, ba, wx, bx, lam)
    return ob, conv_new, h


def _rotate_pairs(x, cos, sin_signed, even_lane):
    width = x.shape[1]
    partner = jnp.where(even_lane, pltpu.roll(x, width - 1, 1), pltpu.roll(x, 1, 1))
    return x * cos + partner * sin_signed


def _retention_head(qh, kh, vh, s_prev, dmat, qdec, kdec, cdec):
    inner = lax.dot_general(qh.astype(BF16), kh.astype(BF16), (((1,), (1,)), ((), ())),
                            preferred_element_type=F32) * dmat
    o = jnp.dot(inner.astype(BF16), vh, preferred_element_type=F32)
    o = o + jnp.dot((qh * qdec).astype(BF16), s_prev.astype(BF16), preferred_element_type=F32)
    s_new = cdec * s_prev + lax.dot_general((kh * kdec).astype(BF16), vh, (((0,), (0,)), ((), ())),
                                            preferred_element_type=F32)
    return o, s_new


def _group_norm_gate(o, gate, gain):
    mu = jnp.mean(o, axis=-1, keepdims=True)
    oc = o - mu
    var = jnp.mean(oc * oc, axis=-1, keepdims=True)
    return jax.nn.silu(gate) * (oc * lax.rsqrt(var + EPS) * gain)


def _ret_prompt_body(cdec_ref, q_ref, k_ref, v_ref, g_ref, cos_ref, sin_ref, dmat_ref, qdec_ref, kdec_ref,
                     gn_ref, o_ref, s_ref):
    @pl.when(pl.program_id(1) == 0)
    def _():
        s_ref[...] = jnp.zeros(s_ref.shape, F32)

    rows, width = q_ref.shape
    even = (lax.broadcasted_iota(jnp.int32, (rows, width), 1) & 1) == 0
    cos = jnp.concatenate([cos_ref[...]] * R_HEADS, axis=1)
    sin = jnp.concatenate([sin_ref[...]] * R_HEADS, axis=1)
    qr = _rotate_pairs(q_ref[...].astype(F32), cos, sin, even)
    kr = _rotate_pairs(k_ref[...].astype(F32), cos, sin, even) * (R_HEAD_DIM ** -0.5)
    v = v_ref[...]
    gate = g_ref[...].astype(F32)
    qdec = qdec_ref[...]
    kdec = kdec_ref[...]
    gn = gn_ref[...]
    for h in range(R_HEADS):
        sl = slice(h * R_HEAD_DIM, (h + 1) * R_HEAD_DIM)
        o, s_new = _retention_head(qr[:, sl], kr[:, sl], v[:, sl], s_ref[0, h], dmat_ref[h],
                                   qdec[:, sl], kdec[:, sl], cdec_ref[h])
        s_ref[0, h] = s_new
        o_ref[:, sl] = _group_norm_gate(o, gate[:, sl], gn[:, sl]).astype(o_ref.dtype)


def _ret_tables(pos, chunk):
    half = R_HEAD_DIM // 2
    inv = 1.0 / (RET_THETA ** jnp.linspace(0.0, 1.0, half, dtype=F32))
    ang = pos.astype(F32)[:, None] * inv[None, :]
    cos = jnp.repeat(jnp.cos(ang), 2, axis=1)
    sin = jnp.sin(ang)
    sin_signed = jnp.stack([-sin, sin], axis=-1).reshape(pos.shape[0], R_HEAD_DIM)
    lg = jnp.log1p(-jnp.exp2(-5.0 - jnp.arange(R_HEADS, dtype=F32)))
    n = jnp.arange(chunk, dtype=F32)
    diff = n[:, None] - n[None, :]
    dmat = jnp.where(diff >= 0, jnp.exp(jnp.maximum(diff, 0.0)[None] * lg[:, None, None]), 0.0)
    qdec = jnp.repeat(jnp.exp((n[:, None] + 1.0) * lg[None, :]), R_HEAD_DIM, axis=1)
    kdec = jnp.repeat(jnp.exp((chunk - 1.0 - n)[:, None] * lg[None, :]), R_HEAD_DIM, axis=1)
    cdec = jnp.exp(chunk * lg)
    return cos, sin_signed, dmat, qdec, kdec, cdec


def _ret_prompt(proj, l, batch, seq, ret_gn, tables, *, name):
    c = RET_CHUNK
    nc = seq // c
    w = BRANCH_W
    cos, sin, dmat, qdec, kdec, cdec = tables

    def col(off):
        return pl.BlockSpec((c, w), lambda b, n: (b * nc + n, off // w))

    const2 = lambda b, n: (0, 0)
    return pl.pallas_call(
        _ret_prompt_body,
        out_shape=(jax.ShapeDtypeStruct((batch * seq, w), BF16),
                   jax.ShapeDtypeStruct((batch, R_HEADS, R_HEAD_DIM, R_HEAD_DIM), F32)),
        grid=(batch, nc),
        in_specs=[_smem_spec(), col(COL_QC), col(COL_KC), col(COL_VC), col(COL_GC),
                  pl.BlockSpec((c, R_HEAD_DIM), lambda b, n: (n, 0)),
                  pl.BlockSpec((c, R_HEAD_DIM), lambda b, n: (n, 0)),
                  pl.BlockSpec((R_HEADS, c, c), lambda b, n: (0, 0, 0)),
                  pl.BlockSpec((c, w), const2), pl.BlockSpec((c, w), const2),
                  pl.BlockSpec((None, 1, w), lambda b, n: (l, 0, 0))],
        out_specs=(pl.BlockSpec((c, w), lambda b, n: (b * nc + n, 0)),
                   pl.BlockSpec((1, R_HEADS, R_HEAD_DIM, R_HEAD_DIM), lambda b, n: (b, 0, 0, 0))),
        compiler_params=_params(("parallel", "arbitrary")),
        name=name,
    )(cdec, proj, proj, proj, proj, cos, sin, dmat, qdec, kdec, ret_gn)


RET_SAMPLE_G = 4


def _ret_sample_body(*refs):
    (cdec_ref, q_ref, k_ref, v_ref, g_ref, cos_ref, sin_ref, dmat_ref, qdec_ref, kdec_ref,
     gn_ref, s_ref) = refs[:12]
    o_ref, so_ref, o_sc = refs[-3:]
    rows, width = q_ref.shape
    even = (lax.broadcasted_iota(jnp.int32, (rows, width), 1) & 1) == 0
    cos = cos_ref[...]
    sin = sin_ref[...]
    qr = _rotate_pairs(q_ref[...].astype(F32), cos, sin, even)
    kr = _rotate_pairs(k_ref[...].astype(F32), cos, sin, even) * (R_HEAD_DIM ** -0.5)
    vf = v_ref[...].astype(F32)
    gate = g_ref[...].astype(F32)
    qdec = qdec_ref[...]
    kdec = kdec_ref[...]
    gn = gn_ref[...]
    for g in range(RET_SAMPLE_G):
        rs = slice(g * SAMPLE_T, (g + 1) * SAMPLE_T)
        for h in range(R_HEADS):
            sl = slice(h * R_HEAD_DIM, (h + 1) * R_HEAD_DIM)
            o, s_new = _retention_head(qr[rs, sl], kr[rs, sl], vf[rs, sl].astype(BF16), s_ref[g, h],
                                       dmat_ref[h], qdec[rs, sl], kdec[rs, sl], cdec_ref[h])
            so_ref[g, h] = s_new
            o_sc[rs, sl] = _group_norm_gate(o, gate[rs, sl], gn[:, sl])
    o_ref[...] = o_sc[...].astype(o_ref.dtype)


def _ret_sample(proj, l, state, prev, ret_gn, tables, *, name):
    nseq = state.shape[1]
    g = RET_SAMPLE_G
    rows = g * SAMPLE_T
    w = BRANCH_W
    cos, sin, dmat, qdec, kdec, cdec = tables

    def col(off):
        return pl.BlockSpec((rows, w), lambda i: (i, off // w))

    const2 = lambda i: (0, 0)
    tab = pl.BlockSpec((rows, w), const2)
    state_spec = pl.BlockSpec((None, g, R_HEADS, R_HEAD_DIM, R_HEAD_DIM), lambda i: (l, i, 0, 0, 0))
    in_specs = [_smem_spec(), col(COL_QC), col(COL_KC), col(COL_VC), col(COL_GC), tab, tab,
                pl.BlockSpec((R_HEADS, SAMPLE_T, SAMPLE_T), lambda i: (0, 0, 0)),
                tab, tab, pl.BlockSpec((None, 1, w), lambda i: (l, 0, 0)), state_spec]
    args = [cdec, proj, proj, proj, proj, cos, sin, dmat, qdec, kdec, ret_gn, state]
    aliases = {}
    if prev is not None:
        in_specs.append(pl.BlockSpec(memory_space=pl.ANY))
        aliases = {len(args): 1}
        args.append(prev)
    return pl.pallas_call(
        _ret_sample_body,
        out_shape=(jax.ShapeDtypeStruct((nseq * SAMPLE_T, w), BF16), jax.ShapeDtypeStruct(state.shape, F32)),
        grid=(nseq // g,),
        in_specs=in_specs,
        out_specs=(pl.BlockSpec((rows, w), lambda i: (i, 0)), state_spec),
        scratch_shapes=[pltpu.VMEM((rows, w), F32)],
        input_output_aliases=aliases,
        compiler_params=_params(("parallel",)),
        name=name,
    )(*args)


def _ret_sample_tables(g):
    pos = PAST_LEN + jnp.arange(SAMPLE_T)
    cos, sin, dmat, qdec, kdec, cdec = _ret_tables(pos, SAMPLE_T)
    tile_rows = lambda x: jnp.tile(x, (g, 1))
    return (tile_rows(jnp.tile(cos, (1, R_HEADS))), tile_rows(jnp.tile(sin, (1, R_HEADS))), dmat,
            tile_rows(qdec), tile_rows(kdec), cdec)


def _cast_once(first, pairs):
    @pl.when(first)
    def _():
        for src, dst in pairs:
            dst[...] = src[...].astype(BF16)


def _softmax_pv(s, v):
    p = jnp.exp(s - jnp.max(s, axis=-1, keepdims=True))
    denom = jnp.sum(p, axis=-1, keepdims=True)
    return jnp.dot(p.astype(BF16), v, preferred_element_type=F32) / denom


def _cross_prompt_body(h_ref, g_ref, wq_ref, wo_ref, k_ref, v_ref, o_ref, wqb_ref, wob_ref):
    _cast_once((pl.program_id(0) == 0) & (pl.program_id(1) == 0), ((wq_ref, wqb_ref), (wo_ref, wob_ref)))
    h = h_ref[...]
    u = _rms(h, g_ref[...]).astype(BF16)
    q = jnp.dot(u, wqb_ref[...], preferred_element_type=F32).astype(BF16)
    heads = []
    for hh in range(X_HEADS):
        sl = slice(hh * X_HEAD_DIM, (hh + 1) * X_HEAD_DIM)
        k = k_ref[0, :, sl].astype(BF16)
        s = lax.dot_general(q[:, sl], k, (((1,), (1,)), ((), ())), preferred_element_type=F32) * (X_HEAD_DIM ** -0.5)
        heads.append(_softmax_pv(s, v_ref[0, :, sl].astype(BF16)).astype(BF16))
    o = jnp.concatenate(heads, axis=1)
    o_ref[...] = h + jnp.dot(o, wob_ref[...], preferred_element_type=F32)


def _cross_prompt(h, g, w_xq, w_xo, l, mem_k, mem_v, batch, seq, *, name):
    d = h.shape[1]
    tq = min(seq, 512)
    nq = seq // tq
    mem = mem_k.shape[1]
    kv = pl.BlockSpec((1, mem, X_W), lambda b, i: (b, 0, 0))
    row = pl.BlockSpec((tq, d), lambda b, i: (b * nq + i, 0))
    return pl.pallas_call(
        _cross_prompt_body,
        out_shape=jax.ShapeDtypeStruct(h.shape, F32),
        grid=(batch, nq),
        in_specs=[row,
                  pl.BlockSpec((None, 1, d), lambda b, i: (l, 0, 0)),
                  pl.BlockSpec((None, d, X_W), lambda b, i: (l, 0, 0)),
                  pl.BlockSpec((None, X_W, d), lambda b, i: (l, 0, 0)),
                  kv, kv],
        out_specs=row,
        scratch_shapes=[pltpu.VMEM((d, X_W), BF16), pltpu.VMEM((X_W, d), BF16)],
        compiler_params=_params(("arbitrary", "arbitrary")),
        name=name,
    )(h, g, w_xq, w_xo, mem_k, mem_v)


CROSS_SAMPLE_G = 8


def _cross_sample_body(h_ref, g_ref, wq_ref, wo_ref, k_ref, v_ref, o_ref, wqb_ref, wob_ref, o_sc):
    _cast_once(pl.program_id(0) == 0, ((wq_ref, wqb_ref), (wo_ref, wob_ref)))
    h = h_ref[...]
    u = _rms(h, g_ref[...]).astype(BF16)
    qf = jnp.dot(u, wqb_ref[...], preferred_element_type=F32)
    rows = X_HEADS * SAMPLE_T
    ncol = k_ref.shape[1]
    row_head = lax.broadcasted_iota(jnp.int32, (rows, ncol), 0) // SAMPLE_T
    col_head = lax.broadcasted_iota(jnp.int32, (rows, ncol), 1) % X_HEADS
    mask = row_head == col_head
    for g in range(CROSS_SAMPLE_G):
        q_g = qf[g * SAMPLE_T:(g + 1) * SAMPLE_T]
        q4 = jnp.concatenate([q_g[:, hh * X_HEAD_DIM:(hh + 1) * X_HEAD_DIM] for hh in range(X_HEADS)],
                             axis=0).astype(BF16)
        s = lax.dot_general(q4, k_ref[g].astype(BF16), (((1,), (1,)), ((), ())), preferred_element_type=F32)
        s = jnp.where(mask, s * (X_HEAD_DIM ** -0.5), NEG_INF)
        o4 = _softmax_pv(s, v_ref[g].astype(BF16))
        o_sc[g * SAMPLE_T:(g + 1) * SAMPLE_T, :] = jnp.concatenate(
            [o4[hh * SAMPLE_T:(hh + 1) * SAMPLE_T] for hh in range(X_HEADS)], axis=1)
    o_ref[...] = h + jnp.dot(o_sc[...].astype(BF16), wob_ref[...], preferred_element_type=F32)


def _cross_sample(h, g, w_xq, w_xo, l, mem_k, mem_v, *, name):
    _, nseq, mh, hd = mem_k.shape
    d = h.shape[1]
    gs = CROSS_SAMPLE_G
    rows = gs * SAMPLE_T
    kv = pl.BlockSpec((None, gs, mh, hd), lambda i: (l, i, 0, 0))
    row = pl.BlockSpec((rows, d), lambda i: (i, 0))
    return pl.pallas_call(
        _cross_sample_body,
        out_shape=jax.ShapeDtypeStruct(h.shape, F32),
        grid=(nseq // gs,),
        in_specs=[row,
                  pl.BlockSpec((None, 1, d), lambda i: (l, 0, 0)),
                  pl.BlockSpec((None, d, X_W), lambda i: (l, 0, 0)),
                  pl.BlockSpec((None, X_W, d), lambda i: (l, 0, 0)),
                  kv, kv],
        out_specs=row,
        scratch_shapes=[pltpu.VMEM((d, X_W), BF16), pltpu.VMEM((X_W, d), BF16), pltpu.VMEM((rows, X_W), F32)],
        compiler_params=_params(("arbitrary",)),
        name=name,
    )(h, g, w_xq, w_xo, mem_k, mem_v)


def _block_diag_groups(w, group_w):
    depth, nb, bs, _ = w.shape
    per = group_w // bs
    w = w.reshape(depth, nb // per, per, bs, bs)
    eye = jnp.eye(per, dtype=w.dtype)
    return jnp.einsum('lcipq,ij->lcipjq', w, eye).reshape(depth, nb // per, group_w, group_w).astype(BF16)


def kernel(x_prompt, x_sample, mem_prompt, cache_win_k, cache_win_v, state_conv, state_lru, state_ret, cache_mem_k, cache_mem_v, norm_mix, w_in, attn_sink, conv_w, conv_b, lru_wa, lru_ba, lru_wx, lru_bx, lru_lambda, ret_gn, w_branch, w_out, norm_cross, w_xq, w_xk, w_xv, w_xo, norm_ffn, w_up, w_down, norm_final):
    bp, tp, d = x_prompt.shape
    bs, ts, _ = x_sample.shape
    depth = w_in.shape[0]
    mlen = mem_prompt.shape[1]
    wb = cache_win_k.shape[2]
    assert ts == SAMPLE_T and wb == WINDOW and w_in.shape[2] == IN_W
    assert tp % WINDOW == 0 and tp % RET_CHUNK == 0

    w_up_b, w_down_b = w_up.astype(BF16), w_down.astype(BF16)
    w_branch_b, w_out_b = w_branch.astype(BF16), w_out.astype(BF16)
    wa_p, wx_p = _block_diag_groups(lru_wa, LRU_GROUP_W), _block_diag_groups(lru_wx, LRU_GROUP_W)
    wa_s, wx_s = _block_diag_groups(lru_wa, LANES), _block_diag_groups(lru_wx, LANES)
    norm_mix3, norm_cross3, norm_ffn3 = map(_layer_vec, (norm_mix, norm_cross, norm_ffn))
    norm_final3 = norm_final.reshape(1, 1, d).astype(F32)
    conv_b3, ba3, bx3, lam3 = map(_layer_vec, (conv_b, lru_ba, lru_bx, lru_lambda))
    ret_gn3 = ret_gn.reshape(depth, 1, BRANCH_W).astype(F32)
    sink = attn_sink.astype(F32)
    win_k_t = cache_win_k.transpose(0, 1, 3, 4, 2)
    win_v_t = cache_win_v.transpose(0, 1, 3, 4, 2)
    conv_t = state_conv.transpose(0, 2, 1, 3)
    mem_k4 = cache_mem_k.reshape(depth, bs, mlen * X_HEADS, X_HEAD_DIM)
    mem_v4 = cache_mem_v.reshape(depth, bs, mlen * X_HEADS, X_HEAD_DIM)
    tables_p = _ret_tables(jnp.arange(tp), RET_CHUNK)
    tables_s = _ret_sample_tables(RET_SAMPLE_G)

    hp = x_prompt.reshape(bp * tp, d)
    hs = x_sample.reshape(bs * ts, d)
    mem2d = mem_prompt.reshape(bp * mlen, d)
    up = _rms_cast(hp, norm_mix3, 0, out_dtype=BF16, name='norm_mix_p')
    us = _rms_cast(hs, norm_mix3, 0, out_dtype=BF16, name='norm_mix_s')
    outs = {k: [] for k in ('p_wk', 'p_wv', 'p_conv', 'p_lru', 'p_ret', 'p_mk', 'p_mv', 's_conv', 's_lru')}
    s_wk = s_wv = s_ret = None
    y_prompt = y_sample = None
    for l in range(depth):
        last = l == depth - 1
        lru_p = (conv_w, conv_b3, wa_p, ba3, wx_p, bx3, lam3)
        lru_s = (conv_w, conv_b3, wa_s, ba3, wx_s, bx3, lam3)
        next_norm = (norm_final3, 0) if last else (norm_mix3, l + 1)

        mk = _ws_matmul(mem2d, w_xk, l, tn=X_W, out_dtype=F32, name=f'mem_k_{l}')
        mv = _ws_matmul(mem2d, w_xv, l, tn=X_W, out_dtype=F32, name=f'mem_v_{l}')
        proj = _in_proj(up, w_in, l, name=f'in_proj_p{l}')
        oa = _swa_prompt(proj, sink, l, bp, tp, name=f'swa_p{l}')
        ob, lru_last = _lru_prompt(proj, l, bp, tp, *lru_p, name=f'lru_p{l}')
        oc, ret_last = _ret_prompt(proj, l, bp, tp, ret_gn3, tables_p, name=f'ret_p{l}')
        hp = _merge_out(oa, ob, oc, w_branch_b, w_out_b, l, proj, hp, name=f'merge_out_p{l}')
        hp = _cross_prompt(hp, norm_cross3, w_xq, w_xo, l, mk.reshape(bp, mlen, X_W), mv.reshape(bp, mlen, X_W),
                           bp, tp, name=f'cross_p{l}')
        res = _ffn(hp, norm_ffn3, w_up_b, w_down_b, l, *next_norm, final=last, name=f'ffn_p{l}')
        if last:
            y_prompt = res
        else:
            hp, up = res
        proj3 = proj.reshape(bp, tp, IN_W)
        outs['p_wk'].append(proj3[:, tp - wb:, COL_KA:COL_KA + A_KV_W].astype(F32).reshape(bp, wb, A_KV_HEADS, A_HEAD_DIM))
        outs['p_wv'].append(proj3[:, tp - wb:, COL_VA:COL_VA + A_KV_W].astype(F32).reshape(bp, wb, A_KV_HEADS, A_HEAD_DIM))
        outs['p_conv'].append(proj3[:, tp - (CONV_WIDTH - 1):, COL_XR:COL_XR + BRANCH_W].astype(F32))
        outs['p_lru'].append(lru_last.reshape(bp, BRANCH_W))
        outs['p_ret'].append(ret_last)
        outs['p_mk'].append(mk.reshape(bp, mlen, X_HEADS, X_HEAD_DIM))
        outs['p_mv'].append(mv.reshape(bp, mlen, X_HEADS, X_HEAD_DIM))

        proj = _in_proj(us, w_in, l, name=f'in_proj_s{l}')
        oa, s_wk, s_wv = _swa_sample(proj, sink, l, win_k_t, win_v_t, s_wk, s_wv, name=f'swa_s{l}')
        ob, conv_new, lru_new = _lru_sample(proj, l, conv_t, state_lru, *lru_s, name=f'lru_s{l}')
        oc, s_ret = _ret_sample(proj, l, state_ret, s_ret, ret_gn3, tables_s, name=f'ret_s{l}')
        hs = _merge_out(oa, ob, oc, w_branch_b, w_out_b, l, proj, hs, name=f'merge_out_s{l}')
        hs = _cross_sample(hs, norm_cross3, w_xq, w_xo, l, mem_k4, mem_v4, name=f'cross_s{l}')
        res = _ffn(hs, norm_ffn3, w_up_b, w_down_b, l, *next_norm, final=last, name=f'ffn_s{l}')
        if last:
            y_sample = res
        else:
            hs, us = res
        outs['s_conv'].append(conv_new)
        outs['s_lru'].append(lru_new)

    st = lambda k: jnp.stack(outs[k])
    return (y_prompt.reshape(bp, tp, d), y_sample.reshape(bs, ts, d),
            st('p_wk'), st('p_wv'), st('p_conv'), st('p_lru'), st('p_ret'), st('p_mk'), st('p_mv'),
            s_wk.transpose(0, 1, 4, 2, 3), s_wv.transpose(0, 1, 4, 2, 3),
            st('s_conv').transpose(0, 2, 1, 3), st('s_lru'), s_ret)
```

```python
import functools

import jax
import jax.numpy as jnp
from jax import lax
from jax.experimental import pallas as pl
from jax.experimental.pallas import tpu as pltpu

F32 = jnp.float32
BF16 = jnp.bfloat16
NEG_INF = -1e30
EPS = 1e-6

V7X_VMEM_BYTES = 64 * 1024 * 1024
VMEM_LIMIT_BYTES = V7X_VMEM_BYTES - 8 * 1024 * 1024
LANES = 128
SUBLANES = 8

BRANCH_W = 1024
A_HEAD_DIM = 64
A_HEADS = 16
A_KV_HEADS = 4
A_GROUP = 4
A_KV_W = 256
WINDOW = 128
PAST_LEN = 8192
LRU_C = 8.0
CONV_WIDTH = 4
LRU_GROUP_W = 256
R_HEADS = 4
R_HEAD_DIM = 256
RET_CHUNK = 128
RET_THETA = 10000.0
X_HEADS = 4
X_HEAD_DIM = 128
X_W = 512

COL_QA, COL_XR, COL_YR, COL_QC, COL_KC, COL_VC, COL_GC, COL_GATES, COL_KA, COL_VA = (
    0, 1024, 2048, 3072, 4096, 5120, 6144, 7168, 13312, 13568)
IN_W = 13824
PERM_BLOCK = 512
ROW_TILE = 1024


def _params(semantics):
    return pltpu.CompilerParams(dimension_semantics=semantics, vmem_limit_bytes=VMEM_LIMIT_BYTES)


def _smem_spec():
    return pl.BlockSpec(memory_space=pltpu.SMEM)


def _rms(x, g):
    y = x * lax.rsqrt(jnp.mean(x * x, axis=-1, keepdims=True) + EPS)
    return y * g


def _layer_vec(v):
    return v.reshape(v.shape[0], 1, v.shape[1]).astype(F32)


def _rms_cast_body(x_ref, g_ref, o_ref):
    o_ref[...] = _rms(x_ref[...], g_ref[...]).astype(o_ref.dtype)


def _rms_cast(x, g, l, *, out_dtype, name):
    m, k = x.shape
    tm = min(m, 512)
    return pl.pallas_call(
        _rms_cast_body,
        out_shape=jax.ShapeDtypeStruct((m, k), out_dtype),
        grid=(m // tm,),
        in_specs=[pl.BlockSpec((tm, k), lambda i: (i, 0)), pl.BlockSpec((None, 1, k), lambda i: (l, 0, 0))],
        out_specs=pl.BlockSpec((tm, k), lambda i: (i, 0)),
        compiler_params=_params(("parallel",)),
        name=name,
    )(x, g)


def _in_proj_body(u_ref, w0_ref, w1_ref, w2_ref, o_ref, wb_ref):
    @pl.when(pl.program_id(1) == 0)
    def _():
        for k, w_ref in enumerate((w0_ref, w1_ref, w2_ref)):
            wb_ref[:, k * PERM_BLOCK:(k + 1) * PERM_BLOCK] = w_ref[...].astype(BF16)

    o_ref[...] = jnp.dot(u_ref[...], wb_ref[...], preferred_element_type=F32).astype(o_ref.dtype)


def _in_proj(u, w_in, l, *, name):
    m, k = u.shape
    tm = min(m, ROW_TILE)
    per = 3
    tn = per * PERM_BLOCK
    nblk = IN_W // PERM_BLOCK

    def src_block(c):
        return jnp.where(c < 2, c, jnp.where(c < nblk - 1, c + 1, 2))

    def w_spec(kk):
        return pl.BlockSpec((None, k, PERM_BLOCK), lambda j, i: (l, 0, src_block(per * j + kk)))

    return pl.pallas_call(
        _in_proj_body,
        out_shape=jax.ShapeDtypeStruct((m, IN_W), BF16),
        grid=(IN_W // tn, m // tm),
        in_specs=[pl.BlockSpec((tm, k), lambda j, i: (i, 0)), w_spec(0), w_spec(1), w_spec(2)],
        out_specs=pl.BlockSpec((tm, tn), lambda j, i: (i, j)),
        scratch_shapes=[pltpu.VMEM((k, tn), BF16)],
        compiler_params=_params(("parallel", "arbitrary")),
        name=name,
    )(u, w_in, w_in, w_in)


def _ws_matmul_body(*refs, norm, residual):
    it = iter(refs)
    x_ref = next(it)
    g_ref = next(it) if norm else None
    w_ref = next(it)
    r_ref = next(it) if residual else None
    o_ref = next(it)
    wb_ref = next(it)

    @pl.when(pl.program_id(1) == 0)
    def _():
        wb_ref[...] = w_ref[...].astype(BF16)

    x = x_ref[...]
    if norm:
        x = _rms(x, g_ref[...])
    acc = jnp.dot(x.astype(BF16), wb_ref[...], preferred_element_type=F32)
    if residual:
        acc = r_ref[...] + acc
    o_ref[...] = acc.astype(o_ref.dtype)


def _ws_matmul(x, w, l, *, tn, out_dtype, g=None, res=None, name):
    m, k = x.shape
    n = w.shape[2]
    tm = min(m, ROW_TILE)
    assert m % tm == 0 and n % tn == 0
    in_specs = [pl.BlockSpec((tm, k), lambda j, i: (i, 0))]
    args = [x]
    if g is not None:
        in_specs.append(pl.BlockSpec((None, 1, k), lambda j, i: (l, 0, 0)))
        args.append(g)
    in_specs.append(pl.BlockSpec((None, k, tn), lambda j, i: (l, 0, j)))
    args.append(w)
    if res is not None:
        in_specs.append(pl.BlockSpec((tm, tn), lambda j, i: (i, j)))
        args.append(res)
    return pl.pallas_call(
        functools.partial(_ws_matmul_body, norm=g is not None, residual=res is not None),
        out_shape=jax.ShapeDtypeStruct((m, n), out_dtype),
        grid=(n // tn, m // tm),
        in_specs=in_specs,
        out_specs=pl.BlockSpec((tm, tn), lambda j, i: (i, j)),
        scratch_shapes=[pltpu.VMEM((k, tn), BF16)],
        compiler_params=_params(("parallel", "arbitrary")),
        name=name,
    )(*args)


MERGE_ROWS = 256


def _merge_out_body(oa_ref, ob_ref, oc_ref, wbr_ref, g0, g1, g2, g3, g4, g5, wout_ref, h_ref, o_ref):
    gates = ((g0, g1), (g2, g3), (g4, g5))
    half = wbr_ref.shape[2] // 2
    parts = []
    for n in range(2):
        sl = slice(n * half, (n + 1) * half)
        acc = None
        for b, o_b in enumerate((oa_ref, ob_ref, oc_ref)):
            proj = jnp.dot(o_b[...], wbr_ref[b, :, sl], preferred_element_type=F32)
            term = jax.nn.sigmoid(gates[b][n][...].astype(F32)) * proj
            acc = term if acc is None else acc + term
        parts.append(acc.astype(BF16))
    merged = jnp.concatenate(parts, axis=1)
    o_ref[...] = h_ref[...] + jnp.dot(merged, wout_ref[...], preferred_element_type=F32)


def _merge_out(oa, ob, oc, w_branch_b, w_out_b, l, proj, h, *, name):
    m, d = h.shape
    tm = min(m, MERGE_ROWS)
    gw = d // 2
    g0 = COL_GATES // gw
    o_spec = pl.BlockSpec((tm, BRANCH_W), lambda i: (i, 0))
    row = pl.BlockSpec((tm, d), lambda i: (i, 0))
    gate_specs = [pl.BlockSpec((tm, gw), functools.partial(lambda i, k: (i, g0 + k), k=k)) for k in range(6)]
    once = pl.Buffered(1)
    return pl.pallas_call(
        _merge_out_body,
        out_shape=jax.ShapeDtypeStruct((m, d), F32),
        grid=(m // tm,),
        in_specs=[o_spec, o_spec, o_spec,
                  pl.BlockSpec((None, 3, BRANCH_W, d), lambda i: (l, 0, 0, 0), pipeline_mode=once),
                  *gate_specs,
                  pl.BlockSpec((None, d, d), lambda i: (l, 0, 0), pipeline_mode=once),
                  row],
        out_specs=row,
        compiler_params=_params(("parallel",)),
        name=name,
    )(oa, ob, oc, w_branch_b, *([proj] * 6), w_out_b, h)


FFN_ROWS = 512
FFN_CHUNK = 1024
FFN_OUT_SPLIT = 4


def _ffn_body(*refs, final):
    if final:
        h_ref, g_ref, wu_ref, wd_ref, gn_ref, n_ref, u_ref, o_ref = refs
    else:
        h_ref, g_ref, wu_ref, wd_ref, gn_ref, o_ref, n_ref, u_ref = refs
    c = pl.program_id(1)

    @pl.when(c == 0)
    def _():
        h = h_ref[...]
        u_ref[...] = _rms(h, g_ref[...]).astype(BF16)
        o_ref[...] = h

    f = jnp.maximum(jnp.dot(u_ref[...], wu_ref[...], preferred_element_type=F32), 0.0)
    f2 = (f * f).astype(BF16)
    tn = o_ref.shape[1] // FFN_OUT_SPLIT
    for n in range(FFN_OUT_SPLIT):
        sl = slice(n * tn, (n + 1) * tn)
        o_ref[:, sl] += jnp.dot(f2, wd_ref[:, sl], preferred_element_type=F32)

    @pl.when(c == pl.num_programs(1) - 1)
    def _():
        n_ref[...] = _rms(o_ref[...], gn_ref[...]).astype(n_ref.dtype)


def _ffn(h, g, w_up_b, w_down_b, l, gn, gn_index, *, final, name):
    m, d = h.shape
    dff = w_up_b.shape[2]
    tm = min(m, FFN_ROWS)
    tc = FFN_CHUNK
    row = pl.BlockSpec((tm, d), lambda i, c: (i, 0))
    if final:
        out_shape = jax.ShapeDtypeStruct((m, d), F32)
        out_specs = row
        scratch = [pltpu.VMEM((tm, d), BF16), pltpu.VMEM((tm, d), F32)]
    else:
        out_shape = (jax.ShapeDtypeStruct((m, d), F32), jax.ShapeDtypeStruct((m, d), BF16))
        out_specs = (row, row)
        scratch = [pltpu.VMEM((tm, d), BF16)]
    return pl.pallas_call(
        functools.partial(_ffn_body, final=final),
        out_shape=out_shape,
        grid=(m // tm, dff // tc),
        in_specs=[row,
                  pl.BlockSpec((None, 1, d), lambda i, c: (l, 0, 0)),
                  pl.BlockSpec((None, d, tc), lambda i, c: (l, 0, c)),
                  pl.BlockSpec((None, tc, d), lambda i, c: (l, c, 0)),
                  pl.BlockSpec((None, 1, d), lambda i, c: (gn_index, 0, 0))],
        out_specs=out_specs,
        scratch_shapes=scratch,
        compiler_params=_params(("parallel", "arbitrary")),
        name=name,
    )(h, g, w_up_b, w_down_b, gn)


def _sink_softmax_pv(s, mask, sink, v, v_contract):
    s = jnp.where(mask, s * (A_HEAD_DIM ** -0.5), NEG_INF)
    m = jnp.maximum(jnp.max(s, axis=-1, keepdims=True), sink)
    p = jnp.exp(s - m)
    denom = jnp.sum(p, axis=-1, keepdims=True) + jnp.exp(sink - m)
    o = lax.dot_general(p.astype(BF16), v, (((1,), (v_contract,)), ((), ())), preferred_element_type=F32)
    return o / denom


def _swa_prompt_body(sink_ref, q_ref, kp_ref, kc_ref, vp_ref, vc_ref, o_ref, *, layer):
    n = pl.program_id(1)
    q = q_ref[...]
    k = jnp.concatenate([kp_ref[...], kc_ref[...]], axis=0)
    v = jnp.concatenate([vp_ref[...], vc_ref[...]], axis=0)
    row = lax.broadcasted_iota(jnp.int32, (WINDOW, 2 * WINDOW), 0)
    rel = lax.broadcasted_iota(jnp.int32, (WINDOW, 2 * WINDOW), 1) - WINDOW
    first_key = jnp.where(n > 0, -WINDOW, 0)
    mask = (rel <= row) & (rel > row - WINDOW) & (rel >= first_key)
    for h in range(A_KV_HEADS):
        kh = k[:, h * A_HEAD_DIM:(h + 1) * A_HEAD_DIM]
        vh = v[:, h * A_HEAD_DIM:(h + 1) * A_HEAD_DIM]
        outs = []
        for g in range(A_GROUP):
            hh = h * A_GROUP + g
            qh = q[:, hh * A_HEAD_DIM:(hh + 1) * A_HEAD_DIM]
            s = lax.dot_general(qh, kh, (((1,), (1,)), ((), ())), preferred_element_type=F32)
            outs.append(_sink_softmax_pv(s, mask, sink_ref[layer, hh], vh, 0))
        o_ref[:, h * A_KV_W:(h + 1) * A_KV_W] = jnp.concatenate(outs, axis=1).astype(o_ref.dtype)


def _swa_prompt(proj, sink, l, batch, seq, *, name):
    nb = seq // WINDOW
    ka, va = COL_KA // A_KV_W, COL_VA // A_KV_W
    kv = (WINDOW, A_KV_W)
    return pl.pallas_call(
        functools.partial(_swa_prompt_body, layer=l),
        out_shape=jax.ShapeDtypeStruct((batch * seq, BRANCH_W), BF16),
        grid=(batch, nb),
        in_specs=[_smem_spec(),
                  pl.BlockSpec((WINDOW, BRANCH_W), lambda b, n: (b * nb + n, COL_QA // BRANCH_W)),
                  pl.BlockSpec(kv, lambda b, n: (b * nb + jnp.maximum(n - 1, 0), ka)),
                  pl.BlockSpec(kv, lambda b, n: (b * nb + n, ka)),
                  pl.BlockSpec(kv, lambda b, n: (b * nb + jnp.maximum(n - 1, 0), va)),
                  pl.BlockSpec(kv, lambda b, n: (b * nb + n, va))],
        out_specs=pl.BlockSpec((WINDOW, BRANCH_W), lambda b, n: (b * nb + n, 0)),
        compiler_params=_params(("parallel", "parallel")),
        name=name,
    )(sink, proj, proj, proj, proj, proj)


SAMPLE_T = 8
SWA_SAMPLE_G = 8


def _swa_sample_body(*refs, layer, aliased):
    sink_ref, q_ref, kn_ref, vn_ref, ck_ref, cv_ref = refs[:6]
    o_ref, cko_ref, cvo_ref, o_sc = refs[-4:]
    qf = q_ref[...].astype(F32)
    knf = kn_ref[...].astype(F32)
    vnf = vn_ref[...].astype(F32)
    hrows = A_GROUP * SAMPLE_T
    rows = A_KV_HEADS * hrows
    t_idx = lax.broadcasted_iota(jnp.int32, (rows, 2 * WINDOW), 0) % SAMPLE_T
    rel = lax.broadcasted_iota(jnp.int32, (rows, 2 * WINDOW), 1) - WINDOW
    mask = (rel <= t_idx) & (rel > t_idx - WINDOW) & (rel < SAMPLE_T)
    head = lax.broadcasted_iota(jnp.int32, (rows, 1), 0) // SAMPLE_T
    sink = jnp.zeros((rows, 1), F32)
    for hh in range(A_HEADS):
        sink = jnp.where(head == hh, sink_ref[layer, hh], sink)
    keep = lax.broadcasted_iota(jnp.int32, (A_KV_HEADS, A_HEAD_DIM, WINDOW), 2) < WINDOW - SAMPLE_T
    pad = jnp.zeros((WINDOW - SAMPLE_T, A_KV_W), F32)
    shift = WINDOW - SAMPLE_T
    for g in range(SWA_SAMPLE_G):
        r0 = g * SAMPLE_T
        knt = jnp.concatenate([knf[r0:r0 + SAMPLE_T], pad], axis=0).T.reshape(A_KV_HEADS, A_HEAD_DIM, WINDOW)
        vnt = jnp.concatenate([vnf[r0:r0 + SAMPLE_T], pad], axis=0).T.reshape(A_KV_HEADS, A_HEAD_DIM, WINDOW)
        ck, cv = ck_ref[g], cv_ref[g]
        cko_ref[g] = jnp.where(keep, pltpu.roll(ck, shift, 2), pltpu.roll(knt, shift, 2))
        cvo_ref[g] = jnp.where(keep, pltpu.roll(cv, shift, 2), pltpu.roll(vnt, shift, 2))
        q_g = qf[r0:r0 + SAMPLE_T]
        scores = []
        for h in range(A_KV_HEADS):
            kk = jnp.concatenate([ck[h], knt[h]], axis=1).astype(BF16)
            q4 = jnp.concatenate(
                [q_g[:, (h * A_GROUP + gg) * A_HEAD_DIM:(h * A_GROUP + gg + 1) * A_HEAD_DIM]
                 for gg in range(A_GROUP)], axis=0).astype(BF16)
            scores.append(jnp.dot(q4, kk, preferred_element_type=F32))
        s = jnp.where(mask, jnp.concatenate(scores, axis=0) * (A_HEAD_DIM ** -0.5), NEG_INF)
        m = jnp.maximum(jnp.max(s, axis=-1, keepdims=True), sink)
        p = jnp.exp(s - m)
        inv = 1.0 / (jnp.sum(p, axis=-1, keepdims=True) + jnp.exp(sink - m))
        p = p.astype(BF16)
        for h in range(A_KV_HEADS):
            vv = jnp.concatenate([cv[h], vnt[h]], axis=1).astype(BF16)
            hr = slice(h * hrows, (h + 1) * hrows)
            o4 = lax.dot_general(p[hr], vv, (((1,), (1,)), ((), ())), preferred_element_type=F32) * inv[hr]
            o_sc[r0:r0 + SAMPLE_T, h * A_KV_W:(h + 1) * A_KV_W] = jnp.concatenate(
                [o4[gg * SAMPLE_T:(gg + 1) * SAMPLE_T] for gg in range(A_GROUP)], axis=1)
    o_ref[...] = o_sc[...].astype(o_ref.dtype)


def _swa_sample(proj, sink, l, cache_k, cache_v, prev_k, prev_v, *, name):
    depth, nseq = cache_k.shape[:2]
    g = SWA_SAMPLE_G
    rows = g * SAMPLE_T
    ka, va = COL_KA // A_KV_W, COL_VA // A_KV_W
    cache_spec = pl.BlockSpec((None, g, A_KV_HEADS, A_HEAD_DIM, WINDOW), lambda i: (l, i, 0, 0, 0))
    cache_shape = jax.ShapeDtypeStruct(cache_k.shape, F32)
    in_specs = [_smem_spec(),
                pl.BlockSpec((rows, BRANCH_W), lambda i: (i, COL_QA // BRANCH_W)),
                pl.BlockSpec((rows, A_KV_W), lambda i: (i, ka)),
                pl.BlockSpec((rows, A_KV_W), lambda i: (i, va)),
                cache_spec, cache_spec]
    args = [sink, proj, proj, proj, cache_k, cache_v]
    aliases = {}
    if prev_k is not None:
        in_specs += [pl.BlockSpec(memory_space=pl.ANY)] * 2
        aliases = {len(args): 1, len(args) + 1: 2}
        args += [prev_k, prev_v]
    return pl.pallas_call(
        functools.partial(_swa_sample_body, layer=l, aliased=prev_k is not None),
        out_shape=(jax.ShapeDtypeStruct((nseq * SAMPLE_T, BRANCH_W), BF16), cache_shape, cache_shape),
        grid=(nseq // g,),
        in_specs=in_specs,
        out_specs=(pl.BlockSpec((rows, BRANCH_W), lambda i: (i, 0)), cache_spec, cache_spec),
        scratch_shapes=[pltpu.VMEM((rows, BRANCH_W), F32)],
        input_output_aliases=aliases,
        compiler_params=_params(("parallel",)),
        name=name,
    )(*args)


def _lru_gates(xc, wa_ref, ba_ref, wx_ref, bx_ref, lam_ref):
    xb = xc.astype(BF16)
    ngroups, gw, _ = wa_ref.shape

    def blockdiag(w_ref):
        return jnp.concatenate(
            [jnp.dot(xb[:, c * gw:(c + 1) * gw], w_ref[c], preferred_element_type=F32)
             for c in range(ngroups)], axis=1)

    r = jax.nn.sigmoid(blockdiag(wa_ref) + ba_ref[...])
    i = jax.nn.sigmoid(blockdiag(wx_ref) + bx_ref[...])
    nl = -lam_ref[...]
    softplus = jnp.maximum(nl, 0.0) + jnp.log1p(jnp.exp(-jnp.abs(nl)))
    log_a = (-LRU_C * r) * softplus
    a = jnp.exp(log_a)
    th = jnp.tanh(log_a)
    mult = jnp.sqrt((-2.0 * th) / (1.0 - th))
    return a, mult, i


LRU_ROWS = 256


def _lru_prompt_body(x_ref, y_ref, cw_ref, cb_ref, wa_ref, ba_ref, wx_ref, bx_ref, lam_ref,
                     o_ref, hlast_ref, xs_ref, a_ref, u_ref, h_ref):
    t = pl.program_id(1)
    rows, width = x_ref.shape

    @pl.when(t == 0)
    def _():
        xs_ref[0:SUBLANES, :] = jnp.zeros((SUBLANES, width), F32)
        h_ref[...] = jnp.zeros((SUBLANES, width), F32)

    x = x_ref[...].astype(F32)
    xs_ref[SUBLANES:SUBLANES + rows, :] = x
    cw = cw_ref[...]
    taps = [xs_ref[SUBLANES - 3 + j:SUBLANES - 3 + j + rows, :] * cw[j:j + 1] for j in range(CONV_WIDTH - 1)]
    taps.append(x * cw[CONV_WIDTH - 1:CONV_WIDTH])
    xc = cb_ref[...] + (((taps[0] + taps[1]) + taps[2]) + taps[3])
    xs_ref[0:SUBLANES, :] = xs_ref[rows:rows + SUBLANES, :]

    a, mult, i = _lru_gates(xc, wa_ref, ba_ref, wx_ref, bx_ref, lam_ref)
    first = (lax.broadcasted_iota(jnp.int32, (rows, 1), 0) == 0) & (t == 0)
    mult = jnp.where(first, 1.0, mult)
    a_ref[...] = a
    u_ref[...] = mult * (i * xc)

    rid = lax.broadcasted_iota(jnp.int32, (SUBLANES, width), 0)

    def tile(j, h):
        off = pl.multiple_of(j * SUBLANES, SUBLANES)
        at = a_ref[pl.ds(off, SUBLANES), :]
        ut = u_ref[pl.ds(off, SUBLANES), :]
        for d in (1, 2, 4):
            keep = rid >= d
            a_sh = jnp.where(keep, pltpu.roll(at, d, 0), 1.0)
            u_sh = jnp.where(keep, pltpu.roll(ut, d, 0), 0.0)
            ut = at * u_sh + ut
            at = at * a_sh
        hh = at * h + ut
        u_ref[pl.ds(off, SUBLANES), :] = hh
        return jnp.broadcast_to(hh[SUBLANES - 1:SUBLANES, :], (SUBLANES, width))

    h = lax.fori_loop(0, rows // SUBLANES, tile, h_ref[...], unroll=2)
    h_ref[...] = h
    hlast_ref[0] = h[0:1, :]
    o_ref[...] = (jax.nn.gelu(y_ref[...].astype(F32)) * u_ref[...]).astype(o_ref.dtype)


def _lru_prompt(proj, l, batch, seq, conv_w, conv_b, wa, ba, wx, bx, lam, *, name):
    rows = min(LRU_ROWS, seq)
    nt = seq // rows
    w = BRANCH_W
    vec = pl.BlockSpec((None, 1, w), lambda b, t: (l, 0, 0))
    gate_w = pl.BlockSpec((None,) + wa.shape[1:], lambda b, t: (l, 0, 0, 0))
    return pl.pallas_call(
        _lru_prompt_body,
        out_shape=(jax.ShapeDtypeStruct((batch * seq, w), BF16), jax.ShapeDtypeStruct((batch, 1, w), F32)),
        grid=(batch, nt),
        in_specs=[pl.BlockSpec((rows, w), lambda b, t: (b * nt + t, COL_XR // w)),
                  pl.BlockSpec((rows, w), lambda b, t: (b * nt + t, COL_YR // w)),
                  pl.BlockSpec((None, CONV_WIDTH, w), lambda b, t: (l, 0, 0)),
                  vec, gate_w, vec, gate_w, vec, vec],
        out_specs=(pl.BlockSpec((rows, w), lambda b, t: (b * nt + t, 0)),
                   pl.BlockSpec((1, 1, w), lambda b, t: (b, 0, 0))),
        scratch_shapes=[pltpu.VMEM((rows + SUBLANES, w), F32), pltpu.VMEM((rows, w), F32),
                        pltpu.VMEM((rows, w), F32), pltpu.VMEM((SUBLANES, w), F32)],
        compiler_params=_params(("parallel", "arbitrary")),
        name=name,
    )(proj, proj, conv_w, conv_b, wa, ba, wx, bx, lam)


def _lru_sample_body(x_ref, y_ref, c_ref, h0_ref, cw_ref, cb_ref, wa_ref, ba_ref,
                     wx_ref, bx_ref, lam_ref, o_ref, co_ref, ho_ref, xs_ref, ys_ref, os_ref):
    nseq = h0_ref.shape[0]
    xs_ref[...] = x_ref[...].astype(F32)
    ys_ref[...] = y_ref[...].astype(F32)

    def step(ref, t):
        return ref[pl.ds(t, nseq, stride=SAMPLE_T), :]

    hist = [c_ref[j] for j in range(CONV_WIDTH - 1)] + [step(xs_ref, t) for t in range(SAMPLE_T)]
    cw = cw_ref[...]
    xcs = []
    for t in range(SAMPLE_T):
        taps = [hist[t + j] * cw[j:j + 1] for j in range(CONV_WIDTH)]
        xcs.append(cb_ref[...] + (((taps[0] + taps[1]) + taps[2]) + taps[3]))
    xc = jnp.concatenate(xcs, axis=0)
    a, mult, i = _lru_gates(xc, wa_ref, ba_ref, wx_ref, bx_ref, lam_ref)
    u = mult * (i * xc)
    h = h0_ref[...]
    for t in range(SAMPLE_T):
        h = a[t * nseq:(t + 1) * nseq] * h + u[t * nseq:(t + 1) * nseq]
        os_ref[pl.ds(t, nseq, stride=SAMPLE_T), :] = jax.nn.gelu(step(ys_ref, t)) * h
    o_ref[...] = os_ref[...].astype(o_ref.dtype)
    ho_ref[...] = h
    for j in range(CONV_WIDTH - 1):
        co_ref[j] = hist[SAMPLE_T + j]


def _lru_sample(proj, l, state_conv_t, state_lru, conv_w, conv_b, wa, ba, wx, bx, lam, *, name):
    _, nseq, w = state_lru.shape
    rows = nseq * SAMPLE_T
    cw = LANES
    nc = w // cw
    assert wa.shape[1:] == (nc, cw, cw)
    vec = pl.BlockSpec((None, 1, cw), lambda c: (l, 0, c))
    gate_w = pl.BlockSpec((None, 1, cw, cw), lambda c: (l, c, 0, 0))
    ob, conv_new, h = pl.pallas_call(
        _lru_sample_body,
        out_shape=(jax.ShapeDtypeStruct((rows, w), BF16),
                   jax.ShapeDtypeStruct((CONV_WIDTH - 1, nseq, w), F32),
                   jax.ShapeDtypeStruct((nseq, w), F32)),
        grid=(nc,),
        in_specs=[pl.BlockSpec((rows, cw), lambda c: (0, COL_XR // cw + c)),
                  pl.BlockSpec((rows, cw), lambda c: (0, COL_YR // cw + c)),
                  pl.BlockSpec((None, CONV_WIDTH - 1, nseq, cw), lambda c: (l, 0, 0, c)),
                  pl.BlockSpec((None, nseq, cw), lambda c: (l, 0, c)),
                  pl.BlockSpec((None, CONV_WIDTH, cw), lambda c: (l, 0, c)),
                  vec, gate_w, vec, gate_w, vec, vec],
        out_specs=(pl.BlockSpec((rows, cw), lambda c: (0, c)),
                   pl.BlockSpec((CONV_WIDTH - 1, nseq, cw), lambda c: (0, 0, c)),
                   pl.BlockSpec((nseq, cw), lambda c: (0, c))),
        scratch_shapes=[pltpu.VMEM((rows, cw), F32), pltpu.VMEM((rows, cw), F32), pltpu.VMEM((rows, cw), F32)],
        compiler_params=_params(("parallel",)),
        name=name,
    )(proj, proj, state_conv_t, state_lru, conv_w, conv_b, wa, ba, wx, bx, lam)
    return ob, conv_new, h


def _rotate_pairs(x, cos, sin_signed, even_lane):
    width = x.shape[1]
    partner = jnp.where(even_lane, pltpu.roll(x, width - 1, 1), pltpu.roll(x, 1, 1))
    return x * cos + partner * sin_signed


def _retention_head(qh, kh, vh, s_prev, dmat, qdec, kdec, cdec):
    inner = lax.dot_general(qh.astype(BF16), kh.astype(BF16), (((1,), (1,)), ((), ())),
                            preferred_element_type=F32) * dmat
    o = jnp.dot(inner.astype(BF16), vh, preferred_element_type=F32)
    o = o + jnp.dot((qh * qdec).astype(BF16), s_prev.astype(BF16), preferred_element_type=F32)
    s_new = cdec * s_prev + lax.dot_general((kh * kdec).astype(BF16), vh, (((0,), (0,)), ((), ())),
                                            preferred_element_type=F32)
    return o, s_new


def _group_norm_gate(o, gate, gain):
    mu = jnp.mean(o, axis=-1, keepdims=True)
    oc = o - mu
    var = jnp.mean(oc * oc, axis=-1, keepdims=True)
    return jax.nn.silu(gate) * (oc * lax.rsqrt(var + EPS) * gain)


def _ret_prompt_body(cdec_ref, q_ref, k_ref, v_ref, g_ref, cos_ref, sin_ref, dmat_ref, qdec_ref, kdec_ref,
                     gn_ref, o_ref, s_ref):
    @pl.when(pl.program_id(1) == 0)
    def _():
        s_ref[...] = jnp.zeros(s_ref.shape, F32)

    rows, width = q_ref.shape
    even = (lax.broadcasted_iota(jnp.int32, (rows, width), 1) & 1) == 0
    cos = jnp.concatenate([cos_ref[...]] * R_HEADS, axis=1)
    sin = jnp.concatenate([sin_ref[...]] * R_HEADS, axis=1)
    qr = _rotate_pairs(q_ref[...].astype(F32), cos, sin, even)
    kr = _rotate_pairs(k_ref[...].astype(F32), cos, sin, even) * (R_HEAD_DIM ** -0.5)
    v = v_ref[...]
    gate = g_ref[...].astype(F32)
    qdec = qdec_ref[...]
    kdec = kdec_ref[...]
    gn = gn_ref[...]
    for h in range(R_HEADS):
        sl = slice(h * R_HEAD_DIM, (h + 1) * R_HEAD_DIM)
        o, s_new = _retention_head(qr[:, sl], kr[:, sl], v[:, sl], s_ref[0, h], dmat_ref[h],
                                   qdec[:, sl], kdec[:, sl], cdec_ref[h])
        s_ref[0, h] = s_new
        o_ref[:, sl] = _group_norm_gate(o, gate[:, sl], gn[:, sl]).astype(o_ref.dtype)


def _ret_tables(pos, chunk):
    half = R_HEAD_DIM // 2
    inv = 1.0 / (RET_THETA ** jnp.linspace(0.0, 1.0, half, dtype=F32))
    ang = pos.astype(F32)[:, None] * jnp.repeat(inv, 2)[None, :]
    cos = jnp.cos(ang)
    sin_signed = jnp.sin(ang) * jnp.tile(jnp.array([-1.0, 1.0], F32), half)[None, :]
    lg = jnp.log1p(-jnp.exp2(-5.0 - jnp.arange(R_HEADS, dtype=F32)))
    n = jnp.arange(chunk, dtype=F32)
    diff = n[:, None] - n[None, :]
    dmat = jnp.where(diff >= 0, jnp.exp(jnp.maximum(diff, 0.0)[None] * lg[:, None, None]), 0.0)
    qdec = jnp.repeat(jnp.exp((n[:, None] + 1.0) * lg[None, :]), R_HEAD_DIM, axis=1)
    kdec = jnp.repeat(jnp.exp((chunk - 1.0 - n)[:, None] * lg[None, :]), R_HEAD_DIM, axis=1)
    cdec = jnp.exp(chunk * lg)
    return cos, sin_signed, dmat, qdec, kdec, cdec


def _ret_prompt(proj, l, batch, seq, ret_gn, tables, *, name):
    c = RET_CHUNK
    nc = seq // c
    w = BRANCH_W
    cos, sin, dmat, qdec, kdec, cdec = tables

    def col(off):
        return pl.BlockSpec((c, w), lambda b, n: (b * nc + n, off // w))

    const2 = lambda b, n: (0, 0)
    return pl.pallas_call(
        _ret_prompt_body,
        out_shape=(jax.ShapeDtypeStruct((batch * seq, w), BF16),
                   jax.ShapeDtypeStruct((batch, R_HEADS, R_HEAD_DIM, R_HEAD_DIM), F32)),
        grid=(batch, nc),
        in_specs=[_smem_spec(), col(COL_QC), col(COL_KC), col(COL_VC), col(COL_GC),
                  pl.BlockSpec((c, R_HEAD_DIM), lambda b, n: (n, 0)),
                  pl.BlockSpec((c, R_HEAD_DIM), lambda b, n: (n, 0)),
                  pl.BlockSpec((R_HEADS, c, c), lambda b, n: (0, 0, 0)),
                  pl.BlockSpec((c, w), const2), pl.BlockSpec((c, w), const2),
                  pl.BlockSpec((None, 1, w), lambda b, n: (l, 0, 0))],
        out_specs=(pl.BlockSpec((c, w), lambda b, n: (b * nc + n, 0)),
                   pl.BlockSpec((1, R_HEADS, R_HEAD_DIM, R_HEAD_DIM), lambda b, n: (b, 0, 0, 0))),
        compiler_params=_params(("parallel", "arbitrary")),
        name=name,
    )(cdec, proj, proj, proj, proj, cos, sin, dmat, qdec, kdec, ret_gn)


RET_SAMPLE_G = 4


def _ret_sample_body(*refs):
    (cdec_ref, q_ref, k_ref, v_ref, g_ref, cos_ref, sin_ref, dmat_ref, qdec_ref, kdec_ref,
     gn_ref, s_ref) = refs[:12]
    o_ref, so_ref, o_sc = refs[-3:]
    rows, width = q_ref.shape
    even = (lax.broadcasted_iota(jnp.int32, (rows, width), 1) & 1) == 0
    cos = cos_ref[...]
    sin = sin_ref[...]
    qr = _rotate_pairs(q_ref[...].astype(F32), cos, sin, even)
    kr = _rotate_pairs(k_ref[...].astype(F32), cos, sin, even) * (R_HEAD_DIM ** -0.5)
    vf = v_ref[...].astype(F32)
    gate = g_ref[...].astype(F32)
    qdec = qdec_ref[...]
    kdec = kdec_ref[...]
    gn = gn_ref[...]
    for g in range(RET_SAMPLE_G):
        rs = slice(g * SAMPLE_T, (g + 1) * SAMPLE_T)
        for h in range(R_HEADS):
            sl = slice(h * R_HEAD_DIM, (h + 1) * R_HEAD_DIM)
            o, s_new = _retention_head(qr[rs, sl], kr[rs, sl], vf[rs, sl].astype(BF16), s_ref[g, h],
                                       dmat_ref[h], qdec[rs, sl], kdec[rs, sl], cdec_ref[h])
            so_ref[g, h] = s_new
            o_sc[rs, sl] = _group_norm_gate(o, gate[rs, sl], gn[:, sl])
    o_ref[...] = o_sc[...].astype(o_ref.dtype)


def _ret_sample(proj, l, state, prev, ret_gn, tables, *, name):
    nseq = state.shape[1]
    g = RET_SAMPLE_G
    rows = g * SAMPLE_T
    w = BRANCH_W
    cos, sin, dmat, qdec, kdec, cdec = tables

    def col(off):
        return pl.BlockSpec((rows, w), lambda i: (i, off // w))

    const2 = lambda i: (0, 0)
    tab = pl.BlockSpec((rows, w), const2)
    state_spec = pl.BlockSpec((None, g, R_HEADS, R_HEAD_DIM, R_HEAD_DIM), lambda i: (l, i, 0, 0, 0))
    in_specs = [_smem_spec(), col(COL_QC), col(COL_KC), col(COL_VC), col(COL_GC), tab, tab,
                pl.BlockSpec((R_HEADS, SAMPLE_T, SAMPLE_T), lambda i: (0, 0, 0)),
                tab, tab, pl.BlockSpec((None, 1, w), lambda i: (l, 0, 0)), state_spec]
    args = [cdec, proj, proj, proj, proj, cos, sin, dmat, qdec, kdec, ret_gn, state]
    aliases = {}
    if prev is not None:
        in_specs.append(pl.BlockSpec(memory_space=pl.ANY))
        aliases = {len(args): 1}
        args.append(prev)
    return pl.pallas_call(
        _ret_sample_body,
        out_shape=(jax.ShapeDtypeStruct((nseq * SAMPLE_T, w), BF16), jax.ShapeDtypeStruct(state.shape, F32)),
        grid=(nseq // g,),
        in_specs=in_specs,
        out_specs=(pl.BlockSpec((rows, w), lambda i: (i, 0)), state_spec),
        scratch_shapes=[pltpu.VMEM((rows, w), F32)],
        input_output_aliases=aliases,
        compiler_params=_params(("parallel",)),
        name=name,
    )(*args)


def _ret_sample_tables(g):
    pos = PAST_LEN + jnp.arange(SAMPLE_T)
    cos, sin, dmat, qdec, kdec, cdec = _ret_tables(pos, SAMPLE_T)
    tile_rows = lambda x: jnp.tile(x, (g, 1))
    return (tile_rows(jnp.tile(cos, (1, R_HEADS))), tile_rows(jnp.tile(sin, (1, R_HEADS))), dmat,
            tile_rows(qdec), tile_rows(kdec), cdec)


def _cast_once(first, pairs):
    @pl.when(first)
    def _():
        for src, dst in pairs:
            dst[...] = src[...].astype(BF16)


def _softmax_pv(s, v):
    p = jnp.exp(s - jnp.max(s, axis=-1, keepdims=True))
    denom = jnp.sum(p, axis=-1, keepdims=True)
    return jnp.dot(p.astype(BF16), v, preferred_element_type=F32) / denom


def _cross_prompt_body(h_ref, g_ref, wq_ref, wo_ref, k_ref, v_ref, o_ref, wqb_ref, wob_ref):
    _cast_once((pl.program_id(0) == 0) & (pl.program_id(1) == 0), ((wq_ref, wqb_ref), (wo_ref, wob_ref)))
    h = h_ref[...]
    u = _rms(h, g_ref[...]).astype(BF16)
    q = jnp.dot(u, wqb_ref[...], preferred_element_type=F32).astype(BF16)
    heads = []
    for hh in range(X_HEADS):
        sl = slice(hh * X_HEAD_DIM, (hh + 1) * X_HEAD_DIM)
        k = k_ref[0, :, sl].astype(BF16)
        s = lax.dot_general(q[:, sl], k, (((1,), (1,)), ((), ())), preferred_element_type=F32) * (X_HEAD_DIM ** -0.5)
        heads.append(_softmax_pv(s, v_ref[0, :, sl].astype(BF16)).astype(BF16))
    o = jnp.concatenate(heads, axis=1)
    o_ref[...] = h + jnp.dot(o, wob_ref[...], preferred_element_type=F32)


def _cross_prompt(h, g, w_xq, w_xo, l, mem_k, mem_v, batch, seq, *, name):
    d = h.shape[1]
    tq = min(seq, 512)
    nq = seq // tq
    mem = mem_k.shape[1]
    kv = pl.BlockSpec((1, mem, X_W), lambda b, i: (b, 0, 0))
    row = pl.BlockSpec((tq, d), lambda b, i: (b * nq + i, 0))
    return pl.pallas_call(
        _cross_prompt_body,
        out_shape=jax.ShapeDtypeStruct(h.shape, F32),
        grid=(batch, nq),
        in_specs=[row,
                  pl.BlockSpec((None, 1, d), lambda b, i: (l, 0, 0)),
                  pl.BlockSpec((None, d, X_W), lambda b, i: (l, 0, 0)),
                  pl.BlockSpec((None, X_W, d), lambda b, i: (l, 0, 0)),
                  kv, kv],
        out_specs=row,
        scratch_shapes=[pltpu.VMEM((d, X_W), BF16), pltpu.VMEM((X_W, d), BF16)],
        compiler_params=_params(("arbitrary", "arbitrary")),
        name=name,
    )(h, g, w_xq, w_xo, mem_k, mem_v)


CROSS_SAMPLE_G = 8


def _cross_sample_body(h_ref, g_ref, wq_ref, wo_ref, k_ref, v_ref, o_ref, wqb_ref, wob_ref, o_sc):
    _cast_once(pl.program_id(0) == 0, ((wq_ref, wqb_ref), (wo_ref, wob_ref)))
    h = h_ref[...]
    u = _rms(h, g_ref[...]).astype(BF16)
    qf = jnp.dot(u, wqb_ref[...], preferred_element_type=F32)
    rows = X_HEADS * SAMPLE_T
    ncol = k_ref.shape[1]
    row_head = lax.broadcasted_iota(jnp.int32, (rows, ncol), 0) // SAMPLE_T
    col_head = lax.broadcasted_iota(jnp.int32, (rows, ncol), 1) % X_HEADS
    mask = row_head == col_head
    for g in range(CROSS_SAMPLE_G):
        q_g = qf[g * SAMPLE_T:(g + 1) * SAMPLE_T]
        q4 = jnp.concatenate([q_g[:, hh * X_HEAD_DIM:(hh + 1) * X_HEAD_DIM] for hh in range(X_HEADS)],
                             axis=0).astype(BF16)
        s = lax.dot_general(q4, k_ref[g].astype(BF16), (((1,), (1,)), ((), ())), preferred_element_type=F32)
        s = jnp.where(mask, s * (X_HEAD_DIM ** -0.5), NEG_INF)
        o4 = _softmax_pv(s, v_ref[g].astype(BF16))
        o_sc[g * SAMPLE_T:(g + 1) * SAMPLE_T, :] = jnp.concatenate(
            [o4[hh * SAMPLE_T:(hh + 1) * SAMPLE_T] for hh in range(X_HEADS)], axis=1)
    o_ref[...] = h + jnp.dot(o_sc[...].astype(BF16), wob_ref[...], preferred_element_type=F32)


def _cross_sample(h, g, w_xq, w_xo, l, mem_k, mem_v, *, name):
    _, nseq, mh, hd = mem_k.shape
    d = h.shape[1]
    gs = CROSS_SAMPLE_G
    rows = gs * SAMPLE_T
    kv = pl.BlockSpec((None, gs, mh, hd), lambda i: (l, i, 0, 0))
    row = pl.BlockSpec((rows, d), lambda i: (i, 0))
    return pl.pallas_call(
        _cross_sample_body,
        out_shape=jax.ShapeDtypeStruct(h.shape, F32),
        grid=(nseq // gs,),
        in_specs=[row,
                  pl.BlockSpec((None, 1, d), lambda i: (l, 0, 0)),
                  pl.BlockSpec((None, d, X_W), lambda i: (l, 0, 0)),
                  pl.BlockSpec((None, X_W, d), lambda i: (l, 0, 0)),
                  kv, kv],
        out_specs=row,
        scratch_shapes=[pltpu.VMEM((d, X_W), BF16), pltpu.VMEM((X_W, d), BF16), pltpu.VMEM((rows, X_W), F32)],
        compiler_params=_params(("arbitrary",)),
        name=name,
    )(h, g, w_xq, w_xo, mem_k, mem_v)


def _block_diag_groups(w, group_w):
    depth, nb, bs, _ = w.shape
    per = group_w // bs
    w = w.reshape(depth, nb // per, per, bs, bs)
    eye = jnp.eye(per, dtype=w.dtype)
    return jnp.einsum('lcipq,ij->lcipjq', w, eye).reshape(depth, nb // per, group_w, group_w).astype(BF16)


def kernel(x_prompt, x_sample, mem_prompt, cache_win_k, cache_win_v, state_conv, state_lru, state_ret, cache_mem_k, cache_mem_v, norm_mix, w_in, attn_sink, conv_w, conv_b, lru_wa, lru_ba, lru_wx, lru_bx, lru_lambda, ret_gn, w_branch, w_out, norm_cross, w_xq, w_xk, w_xv, w_xo, norm_ffn, w_up, w_down, norm_final):
    bp, tp, d = x_prompt.shape
    bs, ts, _ = x_sample.shape
    depth = w_in.shape[0]
    mlen = mem_prompt.shape[1]
    wb = cache_win_k.shape[2]
    assert ts == SAMPLE_T and wb == WINDOW and w_in.shape[2] == IN_W
    assert tp % WINDOW == 0 and tp % RET_CHUNK == 0

    w_up_b, w_down_b = w_up.astype(BF16), w_down.astype(BF16)
    w_branch_b, w_out_b = w_branch.astype(BF16), w_out.astype(BF16)
    wa_p, wx_p = _block_diag_groups(lru_wa, LRU_GROUP_W), _block_diag_groups(lru_wx, LRU_GROUP_W)
    wa_s, wx_s = _block_diag_groups(lru_wa, LANES), _block_diag_groups(lru_wx, LANES)
    norm_mix3, norm_cross3, norm_ffn3 = map(_layer_vec, (norm_mix, norm_cross, norm_ffn))
    norm_final3 = norm_final.reshape(1, 1, d).astype(F32)
    conv_b3, ba3, bx3, lam3 = map(_layer_vec, (conv_b, lru_ba, lru_bx, lru_lambda))
    ret_gn3 = ret_gn.reshape(depth, 1, BRANCH_W).astype(F32)
    sink = attn_sink.astype(F32)
    win_k_t = cache_win_k.transpose(0, 1, 3, 4, 2)
    win_v_t = cache_win_v.transpose(0, 1, 3, 4, 2)
    conv_t = state_conv.transpose(0, 2, 1, 3)
    mem_k4 = cache_mem_k.reshape(depth, bs, mlen * X_HEADS, X_HEAD_DIM)
    mem_v4 = cache_mem_v.reshape(depth, bs, mlen * X_HEADS, X_HEAD_DIM)
    tables_p = _ret_tables(jnp.arange(tp), RET_CHUNK)
    tables_s = _ret_sample_tables(RET_SAMPLE_G)

    hp = x_prompt.reshape(bp * tp, d)
    hs = x_sample.reshape(bs * ts, d)
    mem2d = mem_prompt.reshape(bp * mlen, d)
    up = _rms_cast(hp, norm_mix3, 0, out_dtype=BF16, name='norm_mix_p')
    us = _rms_cast(hs, norm_mix3, 0, out_dtype=BF16, name='norm_mix_s')
    outs = {k: [] for k in ('p_wk', 'p_wv', 'p_conv', 'p_lru', 'p_ret', 'p_mk', 'p_mv', 's_conv', 's_lru')}
    s_wk = s_wv = s_ret = None
    y_prompt = y_sample = None
    for l in range(depth):
        last = l == depth - 1
        lru_p = (conv_w, conv_b3, wa_p, ba3, wx_p, bx3, lam3)
        lru_s = (conv_w, conv_b3, wa_s, ba3, wx_s, bx3, lam3)
        next_norm = (norm_final3, 0) if last else (norm_mix3, l + 1)

        mk = _ws_matmul(mem2d, w_xk, l, tn=X_W, out_dtype=F32, name=f'mem_k_{l}')
        mv = _ws_matmul(mem2d, w_xv, l, tn=X_W, out_dtype=F32, name=f'mem_v_{l}')
        proj = _in_proj(up, w_in, l, name=f'in_proj_p{l}')
        oa = _swa_prompt(proj, sink, l, bp, tp, name=f'swa_p{l}')
        ob, lru_last = _lru_prompt(proj, l, bp, tp, *lru_p, name=f'lru_p{l}')
        oc, ret_last = _ret_prompt(proj, l, bp, tp, ret_gn3, tables_p, name=f'ret_p{l}')
        hp = _merge_out(oa, ob, oc, w_branch_b, w_out_b, l, proj, hp, name=f'merge_out_p{l}')
        hp = _cross_prompt(hp, norm_cross3, w_xq, w_xo, l, mk.reshape(bp, mlen, X_W), mv.reshape(bp, mlen, X_W),
                           bp, tp, name=f'cross_p{l}')
        res = _ffn(hp, norm_ffn3, w_up_b, w_down_b, l, *next_norm, final=last, name=f'ffn_p{l}')
        if last:
            y_prompt = res
        else:
            hp, up = res
        proj3 = proj.reshape(bp, tp, IN_W)
        outs['p_wk'].append(proj3[:, tp - wb:, COL_KA:COL_KA + A_KV_W].astype(F32).reshape(bp, wb, A_KV_HEADS, A_HEAD_DIM))
        outs['p_wv'].append(proj3[:, tp - wb:, COL_VA:COL_VA + A_KV_W].astype(F32).reshape(bp, wb, A_KV_HEADS, A_HEAD_DIM))
        outs['p_conv'].append(proj3[:, tp - (CONV_WIDTH - 1):, COL_XR:COL_XR + BRANCH_W].astype(F32))
        outs['p_lru'].append(lru_last.reshape(bp, BRANCH_W))
        outs['p_ret'].append(ret_last)
        outs['p_mk'].append(mk.reshape(bp, mlen, X_HEADS, X_HEAD_DIM))
        outs['p_mv'].append(mv.reshape(bp, mlen, X_HEADS, X_HEAD_DIM))

        proj = _in_proj(us, w_in, l, name=f'in_proj_s{l}')
        oa, s_wk, s_wv = _swa_sample(proj, sink, l, win_k_t, win_v_t, s_wk, s_wv, name=f'swa_s{l}')
        ob, conv_new, lru_new = _lru_sample(proj, l, conv_t, state_lru, *lru_s, name=f'lru_s{l}')
        oc, s_ret = _ret_sample(proj, l, state_ret, s_ret, ret_gn3, tables_s, name=f'ret_s{l}')
        hs = _merge_out(oa, ob, oc, w_branch_b, w_out_b, l, proj, hs, name=f'merge_out_s{l}')
        hs = _cross_sample(hs, norm_cross3, w_xq, w_xo, l, mem_k4, mem_v4, name=f'cross_s{l}')
        res = _ffn(hs, norm_ffn3, w_up_b, w_down_b, l, *next_norm, final=last, name=f'ffn_s{l}')
        if last:
            y_sample = res
        else:
            hs, us = res
        outs['s_conv'].append(conv_new)
        outs['s_lru'].append(lru_new)

    st = lambda k: jnp.stack(outs[k])
    return (y_prompt.reshape(bp, tp, d), y_sample.reshape(bs, ts, d),
            st('p_wk'), st('p_wv'), st('p_conv'), st('p_lru'), st('p_ret'), st('p_mk'), st('p_mv'),
            s_wk.transpose(0, 1, 4, 2, 3), s_wv.transpose(0, 1, 4, 2, 3),
            st('s_conv').transpose(0, 2, 1, 3), st('s_lru'), s_ret)
```

```python
import functools

import jax
import jax.numpy as jnp
from jax import lax
from jax.experimental import pallas as pl
from jax.experimental.pallas import tpu as pltpu

F32 = jnp.float32
BF16 = jnp.bfloat16
NEG_INF = -1e30
EPS = 1e-6

V7X_VMEM_BYTES = 64 * 1024 * 1024
VMEM_LIMIT_BYTES = V7X_VMEM_BYTES - 8 * 1024 * 1024
LANES = 128
SUBLANES = 8

BRANCH_W = 1024
A_HEAD_DIM = 64
A_HEADS = 16
A_KV_HEADS = 4
A_GROUP = 4
A_KV_W = 256
WINDOW = 128
PAST_LEN = 8192
LRU_C = 8.0
CONV_WIDTH = 4
LRU_GROUP_W = 256
R_HEADS = 4
R_HEAD_DIM = 256
RET_CHUNK = 128
RET_THETA = 10000.0
X_HEADS = 4
X_HEAD_DIM = 128
X_W = 512

COL_QA, COL_XR, COL_YR, COL_QC, COL_KC, COL_VC, COL_GC, COL_GATES, COL_KA, COL_VA = (
    0, 1024, 2048, 3072, 4096, 5120, 6144, 7168, 13312, 13568)
IN_W = 13824
PERM_BLOCK = 512
ROW_TILE = 1024


def _params(semantics):
    return pltpu.CompilerParams(dimension_semantics=semantics, vmem_limit_bytes=VMEM_LIMIT_BYTES)


def _smem_spec():
    return pl.BlockSpec(memory_space=pltpu.SMEM)


def _rms(x, g):
    y = x * lax.rsqrt(jnp.mean(x * x, axis=-1, keepdims=True) + EPS)
    return y * g


def _layer_vec(v):
    return v.reshape(v.shape[0], 1, v.shape[1]).astype(F32)


def _rms_cast_body(x_ref, g_ref, o_ref):
    o_ref[...] = _rms(x_ref[...], g_ref[...]).astype(o_ref.dtype)


def _rms_cast(x, g, l, *, out_dtype, name):
    m, k = x.shape
    tm = min(m, 512)
    return pl.pallas_call(
        _rms_cast_body,
        out_shape=jax.ShapeDtypeStruct((m, k), out_dtype),
        grid=(m // tm,),
        in_specs=[pl.BlockSpec((tm, k), lambda i: (i, 0)), pl.BlockSpec((None, 1, k), lambda i: (l, 0, 0))],
        out_specs=pl.BlockSpec((tm, k), lambda i: (i, 0)),
        compiler_params=_params(("parallel",)),
        name=name,
    )(x, g)


def _in_proj_body(*refs, nconv):
    u_ref, w0_ref, w1_ref, w2_ref = refs[:4]
    src_refs = refs[4:4 + nconv]
    o_ref = refs[4 + nconv]
    dst_refs = refs[5 + nconv:5 + 2 * nconv]
    wb_ref = refs[-1]

    @pl.when(pl.program_id(1) == 0)
    def _():
        for k, w_ref in enumerate((w0_ref, w1_ref, w2_ref)):
            wb_ref[:, k * PERM_BLOCK:(k + 1) * PERM_BLOCK] = w_ref[...].astype(BF16)

    o_ref[...] = jnp.dot(u_ref[...], wb_ref[...], preferred_element_type=F32).astype(o_ref.dtype)
    for src, dst in zip(src_refs, dst_refs):
        dst[...] = src[...].astype(BF16)


def _in_proj(u, w_in, l, *, convert=(), name):
    m, k = u.shape
    tm = min(m, ROW_TILE)
    per = 3
    tn = per * PERM_BLOCK
    nblk = IN_W // PERM_BLOCK
    grid = (IN_W // tn, m // tm)

    def src_block(c):
        return jnp.where(c < 2, c, jnp.where(c < nblk - 1, c + 1, 2))

    def w_spec(kk):
        return pl.BlockSpec((None, k, PERM_BLOCK), lambda j, i: (l, 0, src_block(per * j + kk)))

    nconv_blk = 1
    while nconv_blk * 2 <= grid[0] * grid[1]:
        nconv_blk *= 2
    step_block = lambda j, i: jnp.minimum(j * grid[1] + i, nconv_blk - 1)
    conv_in, conv_out, conv_shapes = [], [], []
    for w in convert:
        rows, cols = w.shape[1:]
        rb = rows // nconv_blk
        assert rows % nconv_blk == 0 and rb % 16 == 0
        conv_in.append(pl.BlockSpec((None, rb, cols), lambda j, i: (l, step_block(j, i), 0)))
        conv_out.append(pl.BlockSpec((rb, cols), lambda j, i: (step_block(j, i), 0)))
        conv_shapes.append(jax.ShapeDtypeStruct((rows, cols), BF16))

    res = pl.pallas_call(
        functools.partial(_in_proj_body, nconv=len(convert)),
        out_shape=(jax.ShapeDtypeStruct((m, IN_W), BF16), *conv_shapes),
        grid=grid,
        in_specs=[pl.BlockSpec((tm, k), lambda j, i: (i, 0)), w_spec(0), w_spec(1), w_spec(2), *conv_in],
        out_specs=(pl.BlockSpec((tm, tn), lambda j, i: (i, j)), *conv_out),
        scratch_shapes=[pltpu.VMEM((k, tn), BF16)],
        compiler_params=_params(("arbitrary", "arbitrary")),
        name=name,
    )(u, w_in, w_in, w_in, *convert)
    return res if convert else res[0]


def _ws_matmul_body(*refs, norm, residual):
    it = iter(refs)
    x_ref = next(it)
    g_ref = next(it) if norm else None
    w_ref = next(it)
    r_ref = next(it) if residual else None
    o_ref = next(it)
    wb_ref = next(it)

    @pl.when(pl.program_id(1) == 0)
    def _():
        wb_ref[...] = w_ref[...].astype(BF16)

    x = x_ref[...]
    if norm:
        x = _rms(x, g_ref[...])
    acc = jnp.dot(x.astype(BF16), wb_ref[...], preferred_element_type=F32)
    if residual:
        acc = r_ref[...] + acc
    o_ref[...] = acc.astype(o_ref.dtype)


def _ws_matmul(x, w, l, *, tn, out_dtype, g=None, res=None, name):
    m, k = x.shape
    n = w.shape[2]
    tm = min(m, ROW_TILE)
    assert m % tm == 0 and n % tn == 0
    in_specs = [pl.BlockSpec((tm, k), lambda j, i: (i, 0))]
    args = [x]
    if g is not None:
        in_specs.append(pl.BlockSpec((None, 1, k), lambda j, i: (l, 0, 0)))
        args.append(g)
    in_specs.append(pl.BlockSpec((None, k, tn), lambda j, i: (l, 0, j)))
    args.append(w)
    if res is not None:
        in_specs.append(pl.BlockSpec((tm, tn), lambda j, i: (i, j)))
        args.append(res)
    return pl.pallas_call(
        functools.partial(_ws_matmul_body, norm=g is not None, residual=res is not None),
        out_shape=jax.ShapeDtypeStruct((m, n), out_dtype),
        grid=(n // tn, m // tm),
        in_specs=in_specs,
        out_specs=pl.BlockSpec((tm, tn), lambda j, i: (i, j)),
        scratch_shapes=[pltpu.VMEM((k, tn), BF16)],
        compiler_params=_params(("parallel", "arbitrary")),
        name=name,
    )(*args)


MERGE_ROWS = 256


def _merge_out_body(oa_ref, ob_ref, oc_ref, wbr_ref, g0, g1, g2, g3, g4, g5, wout_ref, h_ref, o_ref):
    gates = ((g0, g1), (g2, g3), (g4, g5))
    half = wbr_ref.shape[2] // 2
    parts = []
    for n in range(2):
        sl = slice(n * half, (n + 1) * half)
        acc = None
        for b, o_b in enumerate((oa_ref, ob_ref, oc_ref)):
            proj = jnp.dot(o_b[...], wbr_ref[b, :, sl], preferred_element_type=F32)
            term = jax.nn.sigmoid(gates[b][n][...].astype(F32)) * proj
            acc = term if acc is None else acc + term
        parts.append(acc.astype(BF16))
    merged = jnp.concatenate(parts, axis=1)
    o_ref[...] = h_ref[...] + jnp.dot(merged, wout_ref[...], preferred_element_type=F32)


def _merge_out(oa, ob, oc, w_branch_b, w_out_b, proj, h, *, name):
    m, d = h.shape
    tm = min(m, MERGE_ROWS)
    gw = d // 2
    g0 = COL_GATES // gw
    o_spec = pl.BlockSpec((tm, BRANCH_W), lambda i: (i, 0))
    row = pl.BlockSpec((tm, d), lambda i: (i, 0))
    gate_specs = [pl.BlockSpec((tm, gw), functools.partial(lambda i, k: (i, g0 + k), k=k)) for k in range(6)]
    once = pl.Buffered(1)
    return pl.pallas_call(
        _merge_out_body,
        out_shape=jax.ShapeDtypeStruct((m, d), F32),
        grid=(m // tm,),
        in_specs=[o_spec, o_spec, o_spec,
                  pl.BlockSpec((3, BRANCH_W, d), lambda i: (0, 0, 0), pipeline_mode=once),
                  *gate_specs,
                  pl.BlockSpec((d, d), lambda i: (0, 0), pipeline_mode=once),
                  row],
        out_specs=row,
        compiler_params=_params(("parallel",)),
        name=name,
    )(oa, ob, oc, w_branch_b, *([proj] * 6), w_out_b, h)


FFN_ROWS = 512
FFN_CHUNK = 1024
FFN_OUT_SPLIT = 4


def _ffn_body(*refs, final):
    if final:
        h_ref, g_ref, wu_ref, wd_ref, gn_ref, n_ref, u_ref, o_ref = refs
    else:
        h_ref, g_ref, wu_ref, wd_ref, gn_ref, o_ref, n_ref, u_ref = refs
    c = pl.program_id(1)

    @pl.when(c == 0)
    def _():
        h = h_ref[...]
        u_ref[...] = _rms(h, g_ref[...]).astype(BF16)
        o_ref[...] = h

    f = jnp.maximum(jnp.dot(u_ref[...], wu_ref[...], preferred_element_type=F32), 0.0)
    f2 = (f * f).astype(BF16)
    tn = o_ref.shape[1] // FFN_OUT_SPLIT
    for n in range(FFN_OUT_SPLIT):
        sl = slice(n * tn, (n + 1) * tn)
        o_ref[:, sl] += jnp.dot(f2, wd_ref[:, sl], preferred_element_type=F32)

    @pl.when(c == pl.num_programs(1) - 1)
    def _():
        n_ref[...] = _rms(o_ref[...], gn_ref[...]).astype(n_ref.dtype)


def _ffn(h, g, l, w_up_b, w_down_b, gn, gn_index, *, final, name):
    m, d = h.shape
    dff = w_up_b.shape[1]
    tm = min(m, FFN_ROWS)
    tc = FFN_CHUNK
    row = pl.BlockSpec((tm, d), lambda i, c: (i, 0))
    if final:
        out_shape = jax.ShapeDtypeStruct((m, d), F32)
        out_specs = row
        scratch = [pltpu.VMEM((tm, d), BF16), pltpu.VMEM((tm, d), F32)]
    else:
        out_shape = (jax.ShapeDtypeStruct((m, d), F32), jax.ShapeDtypeStruct((m, d), BF16))
        out_specs = (row, row)
        scratch = [pltpu.VMEM((tm, d), BF16)]
    return pl.pallas_call(
        functools.partial(_ffn_body, final=final),
        out_shape=out_shape,
        grid=(m // tm, dff // tc),
        in_specs=[row,
                  pl.BlockSpec((None, 1, d), lambda i, c: (l, 0, 0)),
                  pl.BlockSpec((d, tc), lambda i, c: (0, c)),
                  pl.BlockSpec((tc, d), lambda i, c: (c, 0)),
                  pl.BlockSpec((None, 1, d), lambda i, c: (gn_index, 0, 0))],
        out_specs=out_specs,
        scratch_shapes=scratch,
        compiler_params=_params(("parallel", "arbitrary")),
        name=name,
    )(h, g, w_up_b, w_down_b, gn)


def _sink_softmax_pv(s, mask, sink, v, v_contract):
    s = jnp.where(mask, s * (A_HEAD_DIM ** -0.5), NEG_INF)
    m = jnp.maximum(jnp.max(s, axis=-1, keepdims=True), sink)
    p = jnp.exp(s - m)
    denom = jnp.sum(p, axis=-1, keepdims=True) + jnp.exp(sink - m)
    o = lax.dot_general(p.astype(BF16), v, (((1,), (v_contract,)), ((), ())), preferred_element_type=F32)
    return o / denom


def _swa_prompt_body(sink_ref, q_ref, kp_ref, kc_ref, vp_ref, vc_ref, o_ref, *, layer):
    n = pl.program_id(1)
    q = q_ref[...]
    k = jnp.concatenate([kp_ref[...], kc_ref[...]], axis=0)
    v = jnp.concatenate([vp_ref[...], vc_ref[...]], axis=0)
    row = lax.broadcasted_iota(jnp.int32, (WINDOW, 2 * WINDOW), 0)
    rel = lax.broadcasted_iota(jnp.int32, (WINDOW, 2 * WINDOW), 1) - WINDOW
    first_key = jnp.where(n > 0, -WINDOW, 0)
    mask = (rel <= row) & (rel > row - WINDOW) & (rel >= first_key)
    for h in range(A_KV_HEADS):
        kh = k[:, h * A_HEAD_DIM:(h + 1) * A_HEAD_DIM]
        vh = v[:, h * A_HEAD_DIM:(h + 1) * A_HEAD_DIM]
        outs = []
        for g in range(A_GROUP):
            hh = h * A_GROUP + g
            qh = q[:, hh * A_HEAD_DIM:(hh + 1) * A_HEAD_DIM]
            s = lax.dot_general(qh, kh, (((1,), (1,)), ((), ())), preferred_element_type=F32)
            outs.append(_sink_softmax_pv(s, mask, sink_ref[layer, hh], vh, 0))
        o_ref[:, h * A_KV_W:(h + 1) * A_KV_W] = jnp.concatenate(outs, axis=1).astype(o_ref.dtype)


def _swa_prompt(proj, sink, l, batch, seq, *, name):
    nb = seq // WINDOW
    ka, va = COL_KA // A_KV_W, COL_VA // A_KV_W
    kv = (WINDOW, A_KV_W)
    return pl.pallas_call(
        functools.partial(_swa_prompt_body, layer=l),
        out_shape=jax.ShapeDtypeStruct((batch * seq, BRANCH_W), BF16),
        grid=(batch, nb),
        in_specs=[_smem_spec(),
                  pl.BlockSpec((WINDOW, BRANCH_W), lambda b, n: (b * nb + n, COL_QA // BRANCH_W)),
                  pl.BlockSpec(kv, lambda b, n: (b * nb + jnp.maximum(n - 1, 0), ka)),
                  pl.BlockSpec(kv, lambda b, n: (b * nb + n, ka)),
                  pl.BlockSpec(kv, lambda b, n: (b * nb + jnp.maximum(n - 1, 0), va)),
                  pl.BlockSpec(kv, lambda b, n: (b * nb + n, va))],
        out_specs=pl.BlockSpec((WINDOW, BRANCH_W), lambda b, n: (b * nb + n, 0)),
        compiler_params=_params(("parallel", "parallel")),
        name=name,
    )(sink, proj, proj, proj, proj, proj)


SAMPLE_T = 8
SWA_SAMPLE_G = 8


def _swa_sample_body(*refs, layer, aliased):
    sink_ref, q_ref, kn_ref, vn_ref, ck_ref, cv_ref = refs[:6]
    o_ref, cko_ref, cvo_ref, o_sc = refs[-4:]
    qf = q_ref[...].astype(F32)
    knf = kn_ref[...].astype(F32)
    vnf = vn_ref[...].astype(F32)
    hrows = A_GROUP * SAMPLE_T
    rows = A_KV_HEADS * hrows
    t_idx = lax.broadcasted_iota(jnp.int32, (rows, 2 * WINDOW), 0) % SAMPLE_T
    rel = lax.broadcasted_iota(jnp.int32, (rows, 2 * WINDOW), 1) - WINDOW
    mask = (rel <= t_idx) & (rel > t_idx - WINDOW) & (rel < SAMPLE_T)
    head = lax.broadcasted_iota(jnp.int32, (rows, 1), 0) // SAMPLE_T
    sink = jnp.zeros((rows, 1), F32)
    for hh in range(A_HEADS):
        sink = jnp.where(head == hh, sink_ref[layer, hh], sink)
    keep = lax.broadcasted_iota(jnp.int32, (A_KV_HEADS, A_HEAD_DIM, WINDOW), 2) < WINDOW - SAMPLE_T
    pad = jnp.zeros((WINDOW - SAMPLE_T, A_KV_W), F32)
    shift = WINDOW - SAMPLE_T
    for g in range(SWA_SAMPLE_G):
        r0 = g * SAMPLE_T
        knt = jnp.concatenate([knf[r0:r0 + SAMPLE_T], pad], axis=0).T.reshape(A_KV_HEADS, A_HEAD_DIM, WINDOW)
        vnt = jnp.concatenate([vnf[r0:r0 + SAMPLE_T], pad], axis=0).T.reshape(A_KV_HEADS, A_HEAD_DIM, WINDOW)
        ck, cv = ck_ref[g], cv_ref[g]
        cko_ref[g] = jnp.where(keep, pltpu.roll(ck, shift, 2), pltpu.roll(knt, shift, 2))
        cvo_ref[g] = jnp.where(keep, pltpu.roll(cv, shift, 2), pltpu.roll(vnt, shift, 2))
        q_g = qf[r0:r0 + SAMPLE_T]
        scores = []
        for h in range(A_KV_HEADS):
            kk = jnp.concatenate([ck[h], knt[h]], axis=1).astype(BF16)
            q4 = jnp.concatenate(
                [q_g[:, (h * A_GROUP + gg) * A_HEAD_DIM:(h * A_GROUP + gg + 1) * A_HEAD_DIM]
                 for gg in range(A_GROUP)], axis=0).astype(BF16)
            scores.append(jnp.dot(q4, kk, preferred_element_type=F32))
        s = jnp.where(mask, jnp.concatenate(scores, axis=0) * (A_HEAD_DIM ** -0.5), NEG_INF)
        m = jnp.maximum(jnp.max(s, axis=-1, keepdims=True), sink)
        p = jnp.exp(s - m)
        inv = 1.0 / (jnp.sum(p, axis=-1, keepdims=True) + jnp.exp(sink - m))
        p = p.astype(BF16)
        for h in range(A_KV_HEADS):
            vv = jnp.concatenate([cv[h], vnt[h]], axis=1).astype(BF16)
            hr = slice(h * hrows, (h + 1) * hrows)
            o4 = lax.dot_general(p[hr], vv, (((1,), (1,)), ((), ())), preferred_element_type=F32) * inv[hr]
            o_sc[r0:r0 + SAMPLE_T, h * A_KV_W:(h + 1) * A_KV_W] = jnp.concatenate(
                [o4[gg * SAMPLE_T:(gg + 1) * SAMPLE_T] for gg in range(A_GROUP)], axis=1)
    o_ref[...] = o_sc[...].astype(o_ref.dtype)


def _swa_sample(proj, sink, l, cache_k, cache_v, prev_k, prev_v, *, name):
    depth, nseq = cache_k.shape[:2]
    g = SWA_SAMPLE_G
    rows = g * SAMPLE_T
    ka, va = COL_KA // A_KV_W, COL_VA // A_KV_W
    cache_spec = pl.BlockSpec((None, g, A_KV_HEADS, A_HEAD_DIM, WINDOW), lambda i: (l, i, 0, 0, 0))
    cache_shape = jax.ShapeDtypeStruct(cache_k.shape, F32)
    in_specs = [_smem_spec(),
                pl.BlockSpec((rows, BRANCH_W), lambda i: (i, COL_QA // BRANCH_W)),
                pl.BlockSpec((rows, A_KV_W), lambda i: (i, ka)),
                pl.BlockSpec((rows, A_KV_W), lambda i: (i, va)),
                cache_spec, cache_spec]
    args = [sink, proj, proj, proj, cache_k, cache_v]
    aliases = {}
    if prev_k is not None:
        in_specs += [pl.BlockSpec(memory_space=pl.ANY)] * 2
        aliases = {len(args): 1, len(args) + 1: 2}
        args += [prev_k, prev_v]
    return pl.pallas_call(
        functools.partial(_swa_sample_body, layer=l, aliased=prev_k is not None),
        out_shape=(jax.ShapeDtypeStruct((nseq * SAMPLE_T, BRANCH_W), BF16), cache_shape, cache_shape),
        grid=(nseq // g,),
        in_specs=in_specs,
        out_specs=(pl.BlockSpec((rows, BRANCH_W), lambda i: (i, 0)), cache_spec, cache_spec),
        scratch_shapes=[pltpu.VMEM((rows, BRANCH_W), F32)],
        input_output_aliases=aliases,
        compiler_params=_params(("parallel",)),
        name=name,
    )(*args)


def _lru_gates(xc, wa_ref, ba_ref, wx_ref, bx_ref, lam_ref):
    xb = xc.astype(BF16)
    ngroups, gw, _ = wa_ref.shape

    def blockdiag(w_ref):
        return jnp.concatenate(
            [jnp.dot(xb[:, c * gw:(c + 1) * gw], w_ref[c], preferred_element_type=F32)
             for c in range(ngroups)], axis=1)

    r = jax.nn.sigmoid(blockdiag(wa_ref) + ba_ref[...])
    i = jax.nn.sigmoid(blockdiag(wx_ref) + bx_ref[...])
    nl = -lam_ref[...]
    softplus = jnp.maximum(nl, 0.0) + jnp.log1p(jnp.exp(-jnp.abs(nl)))
    log_a = (-LRU_C * r) * softplus
    a = jnp.exp(log_a)
    th = jnp.tanh(log_a)
    mult = jnp.sqrt((-2.0 * th) / (1.0 - th))
    return a, mult, i


LRU_ROWS = 1024


def _lru_prompt_body(x_ref, y_ref, cw_ref, cb_ref, wa_ref, ba_ref, wx_ref, bx_ref, lam_ref,
                     o_ref, hlast_ref, xs_ref, a_ref, u_ref, h_ref):
    t = pl.program_id(1)
    rows, width = x_ref.shape

    @pl.when(t == 0)
    def _():
        xs_ref[0:SUBLANES, :] = jnp.zeros((SUBLANES, width), F32)
        h_ref[...] = jnp.zeros((SUBLANES, width), F32)

    x = x_ref[...].astype(F32)
    xs_ref[SUBLANES:SUBLANES + rows, :] = x
    cw = cw_ref[...]
    taps = [xs_ref[SUBLANES - 3 + j:SUBLANES - 3 + j + rows, :] * cw[j:j + 1] for j in range(CONV_WIDTH - 1)]
    taps.append(x * cw[CONV_WIDTH - 1:CONV_WIDTH])
    xc = cb_ref[...] + (((taps[0] + taps[1]) + taps[2]) + taps[3])
    xs_ref[0:SUBLANES, :] = xs_ref[rows:rows + SUBLANES, :]

    a, mult, i = _lru_gates(xc, wa_ref, ba_ref, wx_ref, bx_ref, lam_ref)
    first = (lax.broadcasted_iota(jnp.int32, (rows, 1), 0) == 0) & (t == 0)
    mult = jnp.where(first, 1.0, mult)
    a_ref[...] = a
    u_ref[...] = mult * (i * xc)

    rid = lax.broadcasted_iota(jnp.int32, (SUBLANES, width), 0)

    def tile(j, h):
        off = pl.multiple_of(j * SUBLANES, SUBLANES)
        at = a_ref[pl.ds(off, SUBLANES), :]
        ut = u_ref[pl.ds(off, SUBLANES), :]
        for d in (1, 2, 4):
            keep = rid >= d
            a_sh = jnp.where(keep, pltpu.roll(at, d, 0), 1.0)
            u_sh = jnp.where(keep, pltpu.roll(ut, d, 0), 0.0)
            ut = at * u_sh + ut
            at = at * a_sh
        hh = at * h + ut
        u_ref[pl.ds(off, SUBLANES), :] = hh
        return jnp.broadcast_to(hh[SUBLANES - 1:SUBLANES, :], (SUBLANES, width))

    h = lax.fori_loop(0, rows // SUBLANES, tile, h_ref[...], unroll=2)
    h_ref[...] = h
    hlast_ref[0] = h[0:1, :]
    o_ref[...] = (jax.nn.gelu(y_ref[...].astype(F32)) * u_ref[...]).astype(o_ref.dtype)


def _lru_prompt(proj, l, batch, seq, conv_w, conv_b, wa, ba, wx, bx, lam, *, name):
    rows = min(LRU_ROWS, seq)
    nt = seq // rows
    w = BRANCH_W
    vec = pl.BlockSpec((None, 1, w), lambda b, t: (l, 0, 0))
    gate_w = pl.BlockSpec((None,) + wa.shape[1:], lambda b, t: (l, 0, 0, 0))
    return pl.pallas_call(
        _lru_prompt_body,
        out_shape=(jax.ShapeDtypeStruct((batch * seq, w), BF16), jax.ShapeDtypeStruct((batch, 1, w), F32)),
        grid=(batch, nt),
        in_specs=[pl.BlockSpec((rows, w), lambda b, t: (b * nt + t, COL_XR // w)),
                  pl.BlockSpec((rows, w), lambda b, t: (b * nt + t, COL_YR // w)),
                  pl.BlockSpec((None, CONV_WIDTH, w), lambda b, t: (l, 0, 0)),
                  vec, gate_w, vec, gate_w, vec, vec],
        out_specs=(pl.BlockSpec((rows, w), lambda b, t: (b * nt + t, 0)),
                   pl.BlockSpec((1, 1, w), lambda b, t: (b, 0, 0))),
        scratch_shapes=[pltpu.VMEM((rows + SUBLANES, w), F32), pltpu.VMEM((rows, w), F32),
                        pltpu.VMEM((rows, w), F32), pltpu.VMEM((SUBLANES, w), F32)],
        compiler_params=_params(("parallel", "arbitrary")),
        name=name,
    )(proj, proj, conv_w, conv_b, wa, ba, wx, bx, lam)


def _lru_sample_body(x_ref, y_ref, c_ref, h0_ref, cw_ref, cb_ref, wa_ref, ba_ref,
                     wx_ref, bx_ref, lam_ref, o_ref, co_ref, ho_ref, xs_ref, ys_ref, os_ref):
    nseq = h0_ref.shape[0]
    xs_ref[...] = x_ref[...].astype(F32)
    ys_ref[...] = y_ref[...].astype(F32)

    def step(ref, t):
        return ref[pl.ds(t, nseq, stride=SAMPLE_T), :]

    hist = [c_ref[j] for j in range(CONV_WIDTH - 1)] + [step(xs_ref, t) for t in range(SAMPLE_T)]
    cw = cw_ref[...]
    xcs = []
    for t in range(SAMPLE_T):
        taps = [hist[t + j] * cw[j:j + 1] for j in range(CONV_WIDTH)]
        xcs.append(cb_ref[...] + (((taps[0] + taps[1]) + taps[2]) + taps[3]))
    xc = jnp.concatenate(xcs, axis=0)
    a, mult, i = _lru_gates(xc, wa_ref, ba_ref, wx_ref, bx_ref, lam_ref)
    u = mult * (i * xc)
    h = h0_ref[...]
    for t in range(SAMPLE_T):
        h = a[t * nseq:(t + 1) * nseq] * h + u[t * nseq:(t + 1) * nseq]
        os_ref[pl.ds(t, nseq, stride=SAMPLE_T), :] = jax.nn.gelu(step(ys_ref, t)) * h
    o_ref[...] = os_ref[...].astype(o_ref.dtype)
    ho_ref[...] = h
    for j in range(CONV_WIDTH - 1):
        co_ref[j] = hist[SAMPLE_T + j]


def _lru_sample(proj, l, state_conv_t, state_lru, conv_w, conv_b, wa, ba, wx, bx, lam, *, name):
    _, nseq, w = state_lru.shape
    rows = nseq * SAMPLE_T
    cw = LANES
    nc = w // cw
    assert wa.shape[1:] == (nc, cw, cw)
    vec = pl.BlockSpec((None, 1, cw), lambda c: (l, 0, c))
    gate_w = pl.BlockSpec((None, 1, cw, cw), lambda c: (l, c, 0, 0))
    ob, conv_new, h = pl.pallas_call(
        _lru_sample_body,
        out_shape=(jax.ShapeDtypeStruct((rows, w), BF16),
                   jax.ShapeDtypeStruct((CONV_WIDTH - 1, nseq, w), F32),
                   jax.ShapeDtypeStruct((nseq, w), F32)),
        grid=(nc,),
        in_specs=[pl.BlockSpec((rows, cw), lambda c: (0, COL_XR // cw + c)),
                  pl.BlockSpec((rows, cw), lambda c: (0, COL_YR // cw + c)),
                  pl.BlockSpec((None, CONV_WIDTH - 1, nseq, cw), lambda c: (l, 0, 0, c)),
                  pl.BlockSpec((None, nseq, cw), lambda c: (l, 0, c)),
                  pl.BlockSpec((None, CONV_WIDTH, cw), lambda c: (l, 0, c)),
                  vec, gate_w, vec, gate_w, vec, vec],
        out_specs=(pl.BlockSpec((rows, cw), lambda c: (0, c)),
                   pl.BlockSpec((CONV_WIDTH - 1, nseq, cw), lambda c: (0, 0, c)),
                   pl.BlockSpec((nseq, cw), lambda c: (0, c))),
        scratch_shapes=[pltpu.VMEM((rows, cw), F32), pltpu.VMEM((rows, cw), F32), pltpu.VMEM((rows, cw), F32)],
        compiler_params=_params(("parallel",)),
        name=name,
    )(proj, proj, state_conv_t, state_lru, conv_w, conv_b, wa, ba, wx, bx, lam)
    return ob, conv_new, h


def _rotate_pairs(x, cos, sin_signed, even_lane):
    width = x.shape[1]
    partner = jnp.where(even_lane, pltpu.roll(x, width - 1, 1), pltpu.roll(x, 1, 1))
    return x * cos + partner * sin_signed


def _retention_head(qh, kh, vh, s_prev, dmat, qdec, kdec, cdec):
    inner = lax.dot_general(qh.astype(BF16), kh.astype(BF16), (((1,), (1,)), ((), ())),
                            preferred_element_type=F32) * dmat
    o = jnp.dot(inner.astype(BF16), vh, preferred_element_type=F32)
    o = o + jnp.dot((qh * qdec).astype(BF16), s_prev.astype(BF16), preferred_element_type=F32)
    s_new = cdec * s_prev + lax.dot_general((kh * kdec).astype(BF16), vh, (((0,), (0,)), ((), ())),
                                            preferred_element_type=F32)
    return o, s_new


def _group_norm_gate(o, gate, gain):
    mu = jnp.mean(o, axis=-1, keepdims=True)
    oc = o - mu
    var = jnp.mean(oc * oc, axis=-1, keepdims=True)
    return jax.nn.silu(gate) * (oc * lax.rsqrt(var + EPS) * gain)


def _ret_prompt_body(cdec_ref, q_ref, k_ref, v_ref, g_ref, cos_ref, sin_ref, dmat_ref, qdec_ref, kdec_ref,
                     gn_ref, o_ref, s_ref):
    @pl.when(pl.program_id(1) == 0)
    def _():
        s_ref[...] = jnp.zeros(s_ref.shape, F32)

    rows, width = q_ref.shape
    even = (lax.broadcasted_iota(jnp.int32, (rows, width), 1) & 1) == 0
    cos = jnp.concatenate([cos_ref[...]] * R_HEADS, axis=1)
    sin = jnp.concatenate([sin_ref[...]] * R_HEADS, axis=1)
    qr = _rotate_pairs(q_ref[...].astype(F32), cos, sin, even)
    kr = _rotate_pairs(k_ref[...].astype(F32), cos, sin, even) * (R_HEAD_DIM ** -0.5)
    v = v_ref[...]
    gate = g_ref[...].astype(F32)
    qdec = qdec_ref[...]
    kdec = kdec_ref[...]
    gn = gn_ref[...]
    for c in range(rows // RET_CHUNK):
        rs = slice(c * RET_CHUNK, (c + 1) * RET_CHUNK)
        for h in range(R_HEADS):
            sl = slice(h * R_HEAD_DIM, (h + 1) * R_HEAD_DIM)
            o, s_new = _retention_head(qr[rs, sl], kr[rs, sl], v[rs, sl], s_ref[0, h], dmat_ref[h],
                                       qdec[:, sl], kdec[:, sl], cdec_ref[h])
            s_ref[0, h] = s_new
            o_ref[rs, sl] = _group_norm_gate(o, gate[rs, sl], gn[:, sl]).astype(o_ref.dtype)


def _ret_tables(pos, chunk):
    half = R_HEAD_DIM // 2
    inv = 1.0 / (RET_THETA ** jnp.linspace(0.0, 1.0, half, dtype=F32))
    ang = pos.astype(F32)[:, None] * jnp.repeat(inv, 2)[None, :]
    cos = jnp.cos(ang)
    sin_signed = jnp.sin(ang) * jnp.tile(jnp.array([-1.0, 1.0], F32), half)[None, :]
    lg = jnp.log1p(-jnp.exp2(-5.0 - jnp.arange(R_HEADS, dtype=F32)))
    n = jnp.arange(chunk, dtype=F32)
    diff = n[:, None] - n[None, :]
    dmat = jnp.where(diff >= 0, jnp.exp(jnp.maximum(diff, 0.0)[None] * lg[:, None, None]), 0.0)
    qdec = jnp.repeat(jnp.exp((n[:, None] + 1.0) * lg[None, :]), R_HEAD_DIM, axis=1)
    kdec = jnp.repeat(jnp.exp((chunk - 1.0 - n)[:, None] * lg[None, :]), R_HEAD_DIM, axis=1)
    cdec = jnp.exp(chunk * lg)
    return cos, sin_signed, dmat, qdec, kdec, cdec


RET_STEP_CHUNKS = 4


def _ret_prompt(proj, l, batch, seq, ret_gn, tables, *, name):
    c = RET_CHUNK
    rows = min(seq, RET_STEP_CHUNKS * c)
    nc = seq // rows
    w = BRANCH_W
    cos, sin, dmat, qdec, kdec, cdec = tables

    def col(off):
        return pl.BlockSpec((rows, w), lambda b, n: (b * nc + n, off // w))

    const2 = lambda b, n: (0, 0)
    return pl.pallas_call(
        _ret_prompt_body,
        out_shape=(jax.ShapeDtypeStruct((batch * seq, w), BF16),
                   jax.ShapeDtypeStruct((batch, R_HEADS, R_HEAD_DIM, R_HEAD_DIM), F32)),
        grid=(batch, nc),
        in_specs=[_smem_spec(), col(COL_QC), col(COL_KC), col(COL_VC), col(COL_GC),
                  pl.BlockSpec((rows, R_HEAD_DIM), lambda b, n: (n, 0)),
                  pl.BlockSpec((rows, R_HEAD_DIM), lambda b, n: (n, 0)),
                  pl.BlockSpec((R_HEADS, c, c), lambda b, n: (0, 0, 0)),
                  pl.BlockSpec((c, w), const2), pl.BlockSpec((c, w), const2),
                  pl.BlockSpec((None, 1, w), lambda b, n: (l, 0, 0))],
        out_specs=(pl.BlockSpec((rows, w), lambda b, n: (b * nc + n, 0)),
                   pl.BlockSpec((1, R_HEADS, R_HEAD_DIM, R_HEAD_DIM), lambda b, n: (b, 0, 0, 0))),
        compiler_params=_params(("parallel", "arbitrary")),
        name=name,
    )(cdec, proj, proj, proj, proj, cos, sin, dmat, qdec, kdec, ret_gn)


RET_SAMPLE_G = 8


def _ret_sample_body(*refs):
    (cdec_ref, q_ref, k_ref, v_ref, g_ref, cos_ref, sin_ref, dmat_ref, qdec_ref, kdec_ref,
     gn_ref, s_ref) = refs[:12]
    o_ref, so_ref, o_sc = refs[-3:]
    rows, width = q_ref.shape
    even = (lax.broadcasted_iota(jnp.int32, (rows, width), 1) & 1) == 0
    cos = cos_ref[...]
    sin = sin_ref[...]
    qr = _rotate_pairs(q_ref[...].astype(F32), cos, sin, even)
    kr = _rotate_pairs(k_ref[...].astype(F32), cos, sin, even) * (R_HEAD_DIM ** -0.5)
    vf = v_ref[...].astype(F32)
    gate = g_ref[...].astype(F32)
    qdec = qdec_ref[...]
    kdec = kdec_ref[...]
    gn = gn_ref[...]
    for g in range(RET_SAMPLE_G):
        rs = slice(g * SAMPLE_T, (g + 1) * SAMPLE_T)
        for h in range(R_HEADS):
            sl = slice(h * R_HEAD_DIM, (h + 1) * R_HEAD_DIM)
            o, s_new = _retention_head(qr[rs, sl], kr[rs, sl], vf[rs, sl].astype(BF16), s_ref[g, h],
                                       dmat_ref[h], qdec[rs, sl], kdec[rs, sl], cdec_ref[h])
            so_ref[g, h] = s_new
            o_sc[rs, sl] = _group_norm_gate(o, gate[rs, sl], gn[:, sl])
    o_ref[...] = o_sc[...].astype(o_ref.dtype)


def _ret_sample(proj, l, state, prev, ret_gn, tables, *, name):
    nseq = state.shape[1]
    g = RET_SAMPLE_G
    rows = g * SAMPLE_T
    w = BRANCH_W
    cos, sin, dmat, qdec, kdec, cdec = tables

    def col(off):
        return pl.BlockSpec((rows, w), lambda i: (i, off // w))

    const2 = lambda i: (0, 0)
    tab = pl.BlockSpec((rows, w), const2)
    state_spec = pl.BlockSpec((None, g, R_HEADS, R_HEAD_DIM, R_HEAD_DIM), lambda i: (l, i, 0, 0, 0))
    in_specs = [_smem_spec(), col(COL_QC), col(COL_KC), col(COL_VC), col(COL_GC), tab, tab,
                pl.BlockSpec((R_HEADS, SAMPLE_T, SAMPLE_T), lambda i: (0, 0, 0)),
                tab, tab, pl.BlockSpec((None, 1, w), lambda i: (l, 0, 0)), state_spec]
    args = [cdec, proj, proj, proj, proj, cos, sin, dmat, qdec, kdec, ret_gn, state]
    aliases = {}
    if prev is not None:
        in_specs.append(pl.BlockSpec(memory_space=pl.ANY))
        aliases = {len(args): 1}
        args.append(prev)
    return pl.pallas_call(
        _ret_sample_body,
        out_shape=(jax.ShapeDtypeStruct((nseq * SAMPLE_T, w), BF16), jax.ShapeDtypeStruct(state.shape, F32)),
        grid=(nseq // g,),
        in_specs=in_specs,
        out_specs=(pl.BlockSpec((rows, w), lambda i: (i, 0)), state_spec),
        scratch_shapes=[pltpu.VMEM((rows, w), F32)],
        input_output_aliases=aliases,
        compiler_params=_params(("parallel",)),
        name=name,
    )(*args)


def _ret_sample_tables(g):
    pos = PAST_LEN + jnp.arange(SAMPLE_T)
    cos, sin, dmat, qdec, kdec, cdec = _ret_tables(pos, SAMPLE_T)
    tile_rows = lambda x: jnp.tile(x, (g, 1))
    return (tile_rows(jnp.tile(cos, (1, R_HEADS))), tile_rows(jnp.tile(sin, (1, R_HEADS))), dmat,
            tile_rows(qdec), tile_rows(kdec), cdec)


def _cast_once(first, pairs):
    @pl.when(first)
    def _():
        for src, dst in pairs:
            dst[...] = src[...].astype(BF16)


def _softmax_pv(s, v):
    p = jnp.exp(s - jnp.max(s, axis=-1, keepdims=True))
    denom = jnp.sum(p, axis=-1, keepdims=True)
    return jnp.dot(p.astype(BF16), v, preferred_element_type=F32) / denom


def _cross_prompt_body(h_ref, g_ref, wq_ref, wo_ref, k_ref, v_ref, o_ref, wqb_ref, wob_ref):
    _cast_once((pl.program_id(0) == 0) & (pl.program_id(1) == 0), ((wq_ref, wqb_ref), (wo_ref, wob_ref)))
    h = h_ref[...]
    u = _rms(h, g_ref[...]).astype(BF16)
    q = jnp.dot(u, wqb_ref[...], preferred_element_type=F32).astype(BF16)
    heads = []
    for hh in range(X_HEADS):
        sl = slice(hh * X_HEAD_DIM, (hh + 1) * X_HEAD_DIM)
        k = k_ref[0, :, sl].astype(BF16)
        s = lax.dot_general(q[:, sl], k, (((1,), (1,)), ((), ())), preferred_element_type=F32) * (X_HEAD_DIM ** -0.5)
        heads.append(_softmax_pv(s, v_ref[0, :, sl].astype(BF16)).astype(BF16))
    o = jnp.concatenate(heads, axis=1)
    o_ref[...] = h + jnp.dot(o, wob_ref[...], preferred_element_type=F32)


def _cross_prompt(h, g, w_xq, w_xo, l, mem_k, mem_v, batch, seq, *, name):
    d = h.shape[1]
    tq = min(seq, 512)
    nq = seq // tq
    mem = mem_k.shape[1]
    kv = pl.BlockSpec((1, mem, X_W), lambda b, i: (b, 0, 0))
    row = pl.BlockSpec((tq, d), lambda b, i: (b * nq + i, 0))
    return pl.pallas_call(
        _cross_prompt_body,
        out_shape=jax.ShapeDtypeStruct(h.shape, F32),
        grid=(batch, nq),
        in_specs=[row,
                  pl.BlockSpec((None, 1, d), lambda b, i: (l, 0, 0)),
                  pl.BlockSpec((None, d, X_W), lambda b, i: (l, 0, 0)),
                  pl.BlockSpec((None, X_W, d), lambda b, i: (l, 0, 0)),
                  kv, kv],
        out_specs=row,
        scratch_shapes=[pltpu.VMEM((d, X_W), BF16), pltpu.VMEM((X_W, d), BF16)],
        compiler_params=_params(("arbitrary", "arbitrary")),
        name=name,
    )(h, g, w_xq, w_xo, mem_k, mem_v)


CROSS_SAMPLE_G = 8


def _cross_sample_body(h_ref, g_ref, wq_ref, wo_ref, k_ref, v_ref, o_ref, wqb_ref, wob_ref, o_sc):
    _cast_once(pl.program_id(0) == 0, ((wq_ref, wqb_ref), (wo_ref, wob_ref)))
    h = h_ref[...]
    u = _rms(h, g_ref[...]).astype(BF16)
    qf = jnp.dot(u, wqb_ref[...], preferred_element_type=F32)
    rows = X_HEADS * SAMPLE_T
    ncol = k_ref.shape[1]
    row_head = lax.broadcasted_iota(jnp.int32, (rows, ncol), 0) // SAMPLE_T
    col_head = lax.broadcasted_iota(jnp.int32, (rows, ncol), 1) % X_HEADS
    mask = row_head == col_head
    for g in range(CROSS_SAMPLE_G):
        q_g = qf[g * SAMPLE_T:(g + 1) * SAMPLE_T]
        q4 = jnp.concatenate([q_g[:, hh * X_HEAD_DIM:(hh + 1) * X_HEAD_DIM] for hh in range(X_HEADS)],
                             axis=0).astype(BF16)
        s = lax.dot_general(q4, k_ref[g].astype(BF16), (((1,), (1,)), ((), ())), preferred_element_type=F32)
        s = jnp.where(mask, s * (X_HEAD_DIM ** -0.5), NEG_INF)
        o4 = _softmax_pv(s, v_ref[g].astype(BF16))
        o_sc[g * SAMPLE_T:(g + 1) * SAMPLE_T, :] = jnp.concatenate(
            [o4[hh * SAMPLE_T:(hh + 1) * SAMPLE_T] for hh in range(X_HEADS)], axis=1)
    o_ref[...] = h + jnp.dot(o_sc[...].astype(BF16), wob_ref[...], preferred_element_type=F32)


def _cross_sample(h, g, w_xq, w_xo, l, mem_k, mem_v, *, name):
    _, nseq, mh, hd = mem_k.shape
    d = h.shape[1]
    gs = CROSS_SAMPLE_G
    rows = gs * SAMPLE_T
    kv = pl.BlockSpec((None, gs, mh, hd), lambda i: (l, i, 0, 0))
    row = pl.BlockSpec((rows, d), lambda i: (i, 0))
    return pl.pallas_call(
        _cross_sample_body,
        out_shape=jax.ShapeDtypeStruct(h.shape, F32),
        grid=(nseq // gs,),
        in_specs=[row,
                  pl.BlockSpec((None, 1, d), lambda i: (l, 0, 0)),
                  pl.BlockSpec((None, d, X_W), lambda i: (l, 0, 0)),
                  pl.BlockSpec((None, X_W, d), lambda i: (l, 0, 0)),
                  kv, kv],
        out_specs=row,
        scratch_shapes=[pltpu.VMEM((d, X_W), BF16), pltpu.VMEM((X_W, d), BF16), pltpu.VMEM((rows, X_W), F32)],
        compiler_params=_params(("arbitrary",)),
        name=name,
    )(h, g, w_xq, w_xo, mem_k, mem_v)


def _block_diag_groups(w, group_w):
    depth, nb, bs, _ = w.shape
    per = group_w // bs
    w = w.reshape(depth, nb // per, per, bs, bs)
    eye = jnp.eye(per, dtype=w.dtype)
    return jnp.einsum('lcipq,ij->lcipjq', w, eye).reshape(depth, nb // per, group_w, group_w).astype(BF16)


def kernel(x_prompt, x_sample, mem_prompt, cache_win_k, cache_win_v, state_conv, state_lru, state_ret, cache_mem_k, cache_mem_v, norm_mix, w_in, attn_sink, conv_w, conv_b, lru_wa, lru_ba, lru_wx, lru_bx, lru_lambda, ret_gn, w_branch, w_out, norm_cross, w_xq, w_xk, w_xv, w_xo, norm_ffn, w_up, w_down, norm_final):
    bp, tp, d = x_prompt.shape
    bs, ts, _ = x_sample.shape
    depth = w_in.shape[0]
    mlen = mem_prompt.shape[1]
    wb = cache_win_k.shape[2]
    assert ts == SAMPLE_T and wb == WINDOW and w_in.shape[2] == IN_W
    assert tp % WINDOW == 0 and tp % RET_CHUNK == 0

    w_branch2 = w_branch.reshape(depth, -1, w_branch.shape[-1])
    wa_p, wx_p = _block_diag_groups(lru_wa, LRU_GROUP_W), _block_diag_groups(lru_wx, LRU_GROUP_W)
    wa_s, wx_s = _block_diag_groups(lru_wa, LANES), _block_diag_groups(lru_wx, LANES)
    norm_mix3, norm_cross3, norm_ffn3 = map(_layer_vec, (norm_mix, norm_cross, norm_ffn))
    norm_final3 = norm_final.reshape(1, 1, d).astype(F32)
    conv_b3, ba3, bx3, lam3 = map(_layer_vec, (conv_b, lru_ba, lru_bx, lru_lambda))
    ret_gn3 = ret_gn.reshape(depth, 1, BRANCH_W).astype(F32)
    sink = attn_sink.astype(F32)
    win_k_t = cache_win_k.transpose(0, 1, 3, 4, 2)
    win_v_t = cache_win_v.transpose(0, 1, 3, 4, 2)
    conv_t = state_conv.transpose(0, 2, 1, 3)
    mem_k4 = cache_mem_k.reshape(depth, bs, mlen * X_HEADS, X_HEAD_DIM)
    mem_v4 = cache_mem_v.reshape(depth, bs, mlen * X_HEADS, X_HEAD_DIM)
    tables_p = _ret_tables(jnp.arange(tp), RET_CHUNK)
    tables_s = _ret_sample_tables(RET_SAMPLE_G)

    hp = x_prompt.reshape(bp * tp, d)
    hs = x_sample.reshape(bs * ts, d)
    mem2d = mem_prompt.reshape(bp * mlen, d)
    up = _rms_cast(hp, norm_mix3, 0, out_dtype=BF16, name='norm_mix_p')
    us = _rms_cast(hs, norm_mix3, 0, out_dtype=BF16, name='norm_mix_s')
    outs = {k: [] for k in ('p_wk', 'p_wv', 'p_conv', 'p_lru', 'p_ret', 'p_mk', 'p_mv', 's_conv', 's_lru')}
    s_wk = s_wv = s_ret = None
    y_prompt = y_sample = None
    for l in range(depth):
        last = l == depth - 1
        lru_p = (conv_w, conv_b3, wa_p, ba3, wx_p, bx3, lam3)
        lru_s = (conv_w, conv_b3, wa_s, ba3, wx_s, bx3, lam3)
        next_norm = (norm_final3, 0) if last else (norm_mix3, l + 1)

        mk = _ws_matmul(mem2d, w_xk, l, tn=X_W, out_dtype=F32, name=f'mem_k_{l}')
        mv = _ws_matmul(mem2d, w_xv, l, tn=X_W, out_dtype=F32, name=f'mem_v_{l}')
        proj, w_up_b, w_down_b, w_branch_b, w_out_b = _in_proj(
            up, w_in, l, convert=(w_up, w_down, w_branch2, w_out), name=f'in_proj_p{l}')
        w_branch_b = w_branch_b.reshape(w_branch.shape[1:])
        oa = _swa_prompt(proj, sink, l, bp, tp, name=f'swa_p{l}')
        ob, lru_last = _lru_prompt(proj, l, bp, tp, *lru_p, name=f'lru_p{l}')
        oc, ret_last = _ret_prompt(proj, l, bp, tp, ret_gn3, tables_p, name=f'ret_p{l}')
        hp = _merge_out(oa, ob, oc, w_branch_b, w_out_b, proj, hp, name=f'merge_out_p{l}')
        hp = _cross_prompt(hp, norm_cross3, w_xq, w_xo, l, mk.reshape(bp, mlen, X_W), mv.reshape(bp, mlen, X_W),
                           bp, tp, name=f'cross_p{l}')
        res = _ffn(hp, norm_ffn3, l, w_up_b, w_down_b, *next_norm, final=last, name=f'ffn_p{l}')
        if last:
            y_prompt = res
        else:
            hp, up = res

        proj_s = _in_proj(us, w_in, l, name=f'in_proj_s{l}')
        oa, s_wk, s_wv = _swa_sample(proj_s, sink, l, win_k_t, win_v_t, s_wk, s_wv, name=f'swa_s{l}')
        ob, conv_new, lru_new = _lru_sample(proj_s, l, conv_t, state_lru, *lru_s, name=f'lru_s{l}')
        oc, s_ret = _ret_sample(proj_s, l, state_ret, s_ret, ret_gn3, tables_s, name=f'ret_s{l}')
        hs = _merge_out(oa, ob, oc, w_branch_b, w_out_b, proj_s, hs, name=f'merge_out_s{l}')
        hs = _cross_sample(hs, norm_cross3, w_xq, w_xo, l, mem_k4, mem_v4, name=f'cross_s{l}')
        res = _ffn(hs, norm_ffn3, l, w_up_b, w_down_b, *next_norm, final=last, name=f'ffn_s{l}')
        if last:
            y_sample = res
        else:
            hs, us = res
        outs['s_conv'].append(conv_new)
        outs['s_lru'].append(lru_new)

        proj3 = proj.reshape(bp, tp, IN_W)
        outs['p_wk'].append(proj3[:, tp - wb:, COL_KA:COL_KA + A_KV_W].astype(F32).reshape(bp, wb, A_KV_HEADS, A_HEAD_DIM))
        outs['p_wv'].append(proj3[:, tp - wb:, COL_VA:COL_VA + A_KV_W].astype(F32).reshape(bp, wb, A_KV_HEADS, A_HEAD_DIM))
        outs['p_conv'].append(proj3[:, tp - (CONV_WIDTH - 1):, COL_XR:COL_XR + BRANCH_W].astype(F32))
        outs['p_lru'].append(lru_last.reshape(bp, BRANCH_W))
        outs['p_ret'].append(ret_last)
        outs['p_mk'].append(mk.reshape(bp, mlen, X_HEADS, X_HEAD_DIM))
        outs['p_mv'].append(mv.reshape(bp, mlen, X_HEADS, X_HEAD_DIM))

    st = lambda k: jnp.stack(outs[k])
    return (y_prompt.reshape(bp, tp, d), y_sample.reshape(bs, ts, d),
            st('p_wk'), st('p_wv'), st('p_conv'), st('p_lru'), st('p_ret'), st('p_mk'), st('p_mv'),
            s_wk.transpose(0, 1, 4, 2, 3), s_wv.transpose(0, 1, 4, 2, 3),
            st('s_conv').transpose(0, 2, 1, 3), st('s_lru'), s_ret)
```

```python
import functools

import jax
import jax.numpy as jnp
from jax import lax
from jax.experimental import pallas as pl
from jax.experimental.pallas import tpu as pltpu

F32 = jnp.float32
BF16 = jnp.bfloat16
NEG_INF = -1e30
EPS = 1e-6

V7X_VMEM_BYTES = 64 * 1024 * 1024
VMEM_LIMIT_BYTES = V7X_VMEM_BYTES - 8 * 1024 * 1024
LANES = 128
SUBLANES = 8

BRANCH_W = 1024
A_HEAD_DIM = 64
A_HEADS = 16
A_KV_HEADS = 4
A_GROUP = 4
A_KV_W = 256
WINDOW = 128
PAST_LEN = 8192
LRU_C = 8.0
CONV_WIDTH = 4
LRU_GROUP_W = 256
R_HEADS = 4
R_HEAD_DIM = 256
RET_CHUNK = 128
RET_THETA = 10000.0
X_HEADS = 4
X_HEAD_DIM = 128
X_W = 512

COL_QA, COL_XR, COL_YR, COL_QC, COL_KC, COL_VC, COL_GC, COL_GATES, COL_KA, COL_VA = (
    0, 1024, 2048, 3072, 4096, 5120, 6144, 7168, 13312, 13568)
IN_W = 13824
PERM_BLOCK = 512
ROW_TILE = 1024


def _params(semantics):
    return pltpu.CompilerParams(dimension_semantics=semantics, vmem_limit_bytes=VMEM_LIMIT_BYTES)


def _smem_spec():
    return pl.BlockSpec(memory_space=pltpu.SMEM)


def _rms(x, g):
    y = x * lax.rsqrt(jnp.mean(x * x, axis=-1, keepdims=True) + EPS)
    return y * g


def _layer_vec(v):
    return v.reshape(v.shape[0], 1, v.shape[1]).astype(F32)


def _rms_cast_body(x_ref, g_ref, o_ref):
    o_ref[...] = _rms(x_ref[...], g_ref[...]).astype(o_ref.dtype)


def _rms_cast(x, g, l, *, out_dtype, name):
    m, k = x.shape
    tm = min(m, 512)
    return pl.pallas_call(
        _rms_cast_body,
        out_shape=jax.ShapeDtypeStruct((m, k), out_dtype),
        grid=(m // tm,),
        in_specs=[pl.BlockSpec((tm, k), lambda i: (i, 0)), pl.BlockSpec((None, 1, k), lambda i: (l, 0, 0))],
        out_specs=pl.BlockSpec((tm, k), lambda i: (i, 0)),
        compiler_params=_params(("parallel",)),
        name=name,
    )(x, g)


def _side_convert_specs(convert, l, grid):
    nblk = 1
    while nblk * 2 <= grid[0] * grid[1]:
        nblk *= 2
    step_block = lambda a, b: jnp.minimum(a * grid[1] + b, nblk - 1)
    in_specs, out_specs, shapes = [], [], []
    for w in convert:
        rows, cols = w.shape[1:]
        rb = rows // nblk
        assert rows % nblk == 0 and rb % 16 == 0
        in_specs.append(pl.BlockSpec((None, rb, cols), lambda a, b: (l, step_block(a, b), 0)))
        out_specs.append(pl.BlockSpec((rb, cols), lambda a, b: (step_block(a, b), 0)))
        shapes.append(jax.ShapeDtypeStruct((rows, cols), BF16))
    return in_specs, out_specs, shapes


def _in_proj_body(*refs, nconv):
    u_ref, w0_ref, w1_ref, w2_ref = refs[:4]
    src_refs = refs[4:4 + nconv]
    o_ref = refs[4 + nconv]
    dst_refs = refs[5 + nconv:5 + 2 * nconv]
    wb_ref = refs[-1]

    @pl.when(pl.program_id(1) == 0)
    def _():
        for k, w_ref in enumerate((w0_ref, w1_ref, w2_ref)):
            wb_ref[:, k * PERM_BLOCK:(k + 1) * PERM_BLOCK] = w_ref[...].astype(BF16)

    o_ref[...] = jnp.dot(u_ref[...], wb_ref[...], preferred_element_type=F32).astype(o_ref.dtype)
    for src, dst in zip(src_refs, dst_refs):
        dst[...] = src[...].astype(BF16)


def _in_proj(u, w_in, l, *, convert=(), name):
    m, k = u.shape
    tm = min(m, ROW_TILE)
    per = 3
    tn = per * PERM_BLOCK
    nblk = IN_W // PERM_BLOCK
    grid = (IN_W // tn, m // tm)

    def src_block(c):
        return jnp.where(c < 2, c, jnp.where(c < nblk - 1, c + 1, 2))

    def w_spec(kk):
        return pl.BlockSpec((None, k, PERM_BLOCK), lambda j, i: (l, 0, src_block(per * j + kk)))

    conv_in, conv_out, conv_shapes = _side_convert_specs(convert, l, grid)

    res = pl.pallas_call(
        functools.partial(_in_proj_body, nconv=len(convert)),
        out_shape=(jax.ShapeDtypeStruct((m, IN_W), BF16), *conv_shapes),
        grid=grid,
        in_specs=[pl.BlockSpec((tm, k), lambda j, i: (i, 0)), w_spec(0), w_spec(1), w_spec(2), *conv_in],
        out_specs=(pl.BlockSpec((tm, tn), lambda j, i: (i, j)), *conv_out),
        scratch_shapes=[pltpu.VMEM((k, tn), BF16)],
        compiler_params=_params(("arbitrary", "arbitrary")),
        name=name,
    )(u, w_in, w_in, w_in, *convert)
    return res if convert else res[0]


def _ws_matmul_body(*refs, norm, residual):
    it = iter(refs)
    x_ref = next(it)
    g_ref = next(it) if norm else None
    w_ref = next(it)
    r_ref = next(it) if residual else None
    o_ref = next(it)
    wb_ref = next(it)

    @pl.when(pl.program_id(1) == 0)
    def _():
        wb_ref[...] = w_ref[...].astype(BF16)

    x = x_ref[...]
    if norm:
        x = _rms(x, g_ref[...])
    acc = jnp.dot(x.astype(BF16), wb_ref[...], preferred_element_type=F32)
    if residual:
        acc = r_ref[...] + acc
    o_ref[...] = acc.astype(o_ref.dtype)


def _ws_matmul(x, w, l, *, tn, out_dtype, g=None, res=None, name):
    m, k = x.shape
    n = w.shape[2]
    tm = min(m, ROW_TILE)
    assert m % tm == 0 and n % tn == 0
    in_specs = [pl.BlockSpec((tm, k), lambda j, i: (i, 0))]
    args = [x]
    if g is not None:
        in_specs.append(pl.BlockSpec((None, 1, k), lambda j, i: (l, 0, 0)))
        args.append(g)
    in_specs.append(pl.BlockSpec((None, k, tn), lambda j, i: (l, 0, j)))
    args.append(w)
    if res is not None:
        in_specs.append(pl.BlockSpec((tm, tn), lambda j, i: (i, j)))
        args.append(res)
    return pl.pallas_call(
        functools.partial(_ws_matmul_body, norm=g is not None, residual=res is not None),
        out_shape=jax.ShapeDtypeStruct((m, n), out_dtype),
        grid=(n // tn, m // tm),
        in_specs=in_specs,
        out_specs=pl.BlockSpec((tm, tn), lambda j, i: (i, j)),
        scratch_shapes=[pltpu.VMEM((k, tn), BF16)],
        compiler_params=_params(("parallel", "arbitrary")),
        name=name,
    )(*args)


MERGE_ROWS = 256


def _merge_out_body(oa_ref, ob_ref, oc_ref, wbr_ref, g0, g1, g2, g3, g4, g5, wout_ref, h_ref, o_ref):
    gates = ((g0, g1), (g2, g3), (g4, g5))
    half = wbr_ref.shape[2] // 2
    parts = []
    for n in range(2):
        sl = slice(n * half, (n + 1) * half)
        acc = None
        for b, o_b in enumerate((oa_ref, ob_ref, oc_ref)):
            proj = jnp.dot(o_b[...], wbr_ref[b, :, sl], preferred_element_type=F32)
            term = jax.nn.sigmoid(gates[b][n][...].astype(F32)) * proj
            acc = term if acc is None else acc + term
        parts.append(acc.astype(BF16))
    merged = jnp.concatenate(parts, axis=1)
    o_ref[...] = h_ref[...] + jnp.dot(merged, wout_ref[...], preferred_element_type=F32)


def _merge_out(oa, ob, oc, w_branch_b, w_out_b, proj, h, *, name):
    m, d = h.shape
    tm = min(m, MERGE_ROWS)
    gw = d // 2
    g0 = COL_GATES // gw
    o_spec = pl.BlockSpec((tm, BRANCH_W), lambda i: (i, 0))
    row = pl.BlockSpec((tm, d), lambda i: (i, 0))
    gate_specs = [pl.BlockSpec((tm, gw), functools.partial(lambda i, k: (i, g0 + k), k=k)) for k in range(6)]
    once = pl.Buffered(1)
    return pl.pallas_call(
        _merge_out_body,
        out_shape=jax.ShapeDtypeStruct((m, d), F32),
        grid=(m // tm,),
        in_specs=[o_spec, o_spec, o_spec,
                  pl.BlockSpec((3, BRANCH_W, d), lambda i: (0, 0, 0), pipeline_mode=once),
                  *gate_specs,
                  pl.BlockSpec((d, d), lambda i: (0, 0), pipeline_mode=once),
                  row],
        out_specs=row,
        compiler_params=_params(("parallel",)),
        name=name,
    )(oa, ob, oc, w_branch_b, *([proj] * 6), w_out_b, h)


FFN_ROWS = 512
FFN_CHUNK = 1024
FFN_OUT_SPLIT = 4


def _ffn_body(*refs, final):
    if final:
        h_ref, g_ref, wu_ref, wd_ref, gn_ref, n_ref, u_ref, o_ref = refs
    else:
        h_ref, g_ref, wu_ref, wd_ref, gn_ref, o_ref, n_ref, u_ref = refs
    c = pl.program_id(1)

    @pl.when(c == 0)
    def _():
        h = h_ref[...]
        u_ref[...] = _rms(h, g_ref[...]).astype(BF16)
        o_ref[...] = h

    f = jnp.maximum(jnp.dot(u_ref[...], wu_ref[...], preferred_element_type=F32), 0.0)
    f2 = (f * f).astype(BF16)
    tn = o_ref.shape[1] // FFN_OUT_SPLIT
    for n in range(FFN_OUT_SPLIT):
        sl = slice(n * tn, (n + 1) * tn)
        o_ref[:, sl] += jnp.dot(f2, wd_ref[:, sl], preferred_element_type=F32)

    @pl.when(c == pl.num_programs(1) - 1)
    def _():
        n_ref[...] = _rms(o_ref[...], gn_ref[...]).astype(n_ref.dtype)


def _ffn(h, g, l, w_up_b, w_down_b, gn, gn_index, *, final, name):
    m, d = h.shape
    dff = w_up_b.shape[1]
    tm = min(m, FFN_ROWS)
    tc = FFN_CHUNK
    row = pl.BlockSpec((tm, d), lambda i, c: (i, 0))
    if final:
        out_shape = jax.ShapeDtypeStruct((m, d), F32)
        out_specs = row
        scratch = [pltpu.VMEM((tm, d), BF16), pltpu.VMEM((tm, d), F32)]
    else:
        out_shape = (jax.ShapeDtypeStruct((m, d), F32), jax.ShapeDtypeStruct((m, d), BF16))
        out_specs = (row, row)
        scratch = [pltpu.VMEM((tm, d), BF16)]
    return pl.pallas_call(
        functools.partial(_ffn_body, final=final),
        out_shape=out_shape,
        grid=(m // tm, dff // tc),
        in_specs=[row,
                  pl.BlockSpec((None, 1, d), lambda i, c: (l, 0, 0)),
                  pl.BlockSpec((d, tc), lambda i, c: (0, c)),
                  pl.BlockSpec((tc, d), lambda i, c: (c, 0)),
                  pl.BlockSpec((None, 1, d), lambda i, c: (gn_index, 0, 0))],
        out_specs=out_specs,
        scratch_shapes=scratch,
        compiler_params=_params(("parallel", "arbitrary")),
        name=name,
    )(h, g, w_up_b, w_down_b, gn)


def _sink_softmax_pv(s, mask, sink, v, v_contract):
    s = jnp.where(mask, s * (A_HEAD_DIM ** -0.5), NEG_INF)
    m = jnp.maximum(jnp.max(s, axis=-1, keepdims=True), sink)
    p = jnp.exp(s - m)
    denom = jnp.sum(p, axis=-1, keepdims=True) + jnp.exp(sink - m)
    o = lax.dot_general(p.astype(BF16), v, (((1,), (v_contract,)), ((), ())), preferred_element_type=F32)
    return o / denom


def _swa_prompt_body(sink_ref, q_ref, kp_ref, kc_ref, vp_ref, vc_ref, o_ref, *, layer):
    n = pl.program_id(1)
    q = q_ref[...]
    k = jnp.concatenate([kp_ref[...], kc_ref[...]], axis=0)
    v = jnp.concatenate([vp_ref[...], vc_ref[...]], axis=0)
    row = lax.broadcasted_iota(jnp.int32, (WINDOW, 2 * WINDOW), 0)
    rel = lax.broadcasted_iota(jnp.int32, (WINDOW, 2 * WINDOW), 1) - WINDOW
    first_key = jnp.where(n > 0, -WINDOW, 0)
    mask = (rel <= row) & (rel > row - WINDOW) & (rel >= first_key)
    for h in range(A_KV_HEADS):
        kh = k[:, h * A_HEAD_DIM:(h + 1) * A_HEAD_DIM]
        vh = v[:, h * A_HEAD_DIM:(h + 1) * A_HEAD_DIM]
        outs = []
        for g in range(A_GROUP):
            hh = h * A_GROUP + g
            qh = q[:, hh * A_HEAD_DIM:(hh + 1) * A_HEAD_DIM]
            s = lax.dot_general(qh, kh, (((1,), (1,)), ((), ())), preferred_element_type=F32)
            outs.append(_sink_softmax_pv(s, mask, sink_ref[layer, hh], vh, 0))
        o_ref[:, h * A_KV_W:(h + 1) * A_KV_W] = jnp.concatenate(outs, axis=1).astype(o_ref.dtype)


def _swa_prompt(proj, sink, l, batch, seq, *, name):
    nb = seq // WINDOW
    ka, va = COL_KA // A_KV_W, COL_VA // A_KV_W
    kv = (WINDOW, A_KV_W)
    return pl.pallas_call(
        functools.partial(_swa_prompt_body, layer=l),
        out_shape=jax.ShapeDtypeStruct((batch * seq, BRANCH_W), BF16),
        grid=(batch, nb),
        in_specs=[_smem_spec(),
                  pl.BlockSpec((WINDOW, BRANCH_W), lambda b, n: (b * nb + n, COL_QA // BRANCH_W)),
                  pl.BlockSpec(kv, lambda b, n: (b * nb + jnp.maximum(n - 1, 0), ka)),
                  pl.BlockSpec(kv, lambda b, n: (b * nb + n, ka)),
                  pl.BlockSpec(kv, lambda b, n: (b * nb + jnp.maximum(n - 1, 0), va)),
                  pl.BlockSpec(kv, lambda b, n: (b * nb + n, va))],
        out_specs=pl.BlockSpec((WINDOW, BRANCH_W), lambda b, n: (b * nb + n, 0)),
        compiler_params=_params(("parallel", "parallel")),
        name=name,
    )(sink, proj, proj, proj, proj, proj)


SAMPLE_T = 8
SWA_SAMPLE_G = 8


def _swa_sample_body(*refs, layer, aliased):
    sink_ref, q_ref, kn_ref, vn_ref, ck_ref, cv_ref = refs[:6]
    o_ref, cko_ref, cvo_ref, o_sc = refs[-4:]
    qf = q_ref[...].astype(F32)
    knf = kn_ref[...].astype(F32)
    vnf = vn_ref[...].astype(F32)
    hrows = A_GROUP * SAMPLE_T
    rows = A_KV_HEADS * hrows
    t_idx = lax.broadcasted_iota(jnp.int32, (rows, 2 * WINDOW), 0) % SAMPLE_T
    rel = lax.broadcasted_iota(jnp.int32, (rows, 2 * WINDOW), 1) - WINDOW
    mask = (rel <= t_idx) & (rel > t_idx - WINDOW) & (rel < SAMPLE_T)
    head = lax.broadcasted_iota(jnp.int32, (rows, 1), 0) // SAMPLE_T
    sink = jnp.zeros((rows, 1), F32)
    for hh in range(A_HEADS):
        sink = jnp.where(head == hh, sink_ref[layer, hh], sink)
    keep = lax.broadcasted_iota(jnp.int32, (A_KV_HEADS, A_HEAD_DIM, WINDOW), 2) < WINDOW - SAMPLE_T
    pad = jnp.zeros((WINDOW - SAMPLE_T, A_KV_W), F32)
    shift = WINDOW - SAMPLE_T
    for g in range(SWA_SAMPLE_G):
        r0 = g * SAMPLE_T
        knt = jnp.concatenate([knf[r0:r0 + SAMPLE_T], pad], axis=0).T.reshape(A_KV_HEADS, A_HEAD_DIM, WINDOW)
        vnt = jnp.concatenate([vnf[r0:r0 + SAMPLE_T], pad], axis=0).T.reshape(A_KV_HEADS, A_HEAD_DIM, WINDOW)
        ck, cv = ck_ref[g], cv_ref[g]
        cko_ref[g] = jnp.where(keep, pltpu.roll(ck, shift, 2), pltpu.roll(knt, shift, 2))
        cvo_ref[g] = jnp.where(keep, pltpu.roll(cv, shift, 2), pltpu.roll(vnt, shift, 2))
        q_g = qf[r0:r0 + SAMPLE_T]
        scores = []
        for h in range(A_KV_HEADS):
            kk = jnp.concatenate([ck[h], knt[h]], axis=1).astype(BF16)
            q4 = jnp.concatenate(
                [q_g[:, (h * A_GROUP + gg) * A_HEAD_DIM:(h * A_GROUP + gg + 1) * A_HEAD_DIM]
                 for gg in range(A_GROUP)], axis=0).astype(BF16)
            scores.append(jnp.dot(q4, kk, preferred_element_type=F32))
        s = jnp.where(mask, jnp.concatenate(scores, axis=0) * (A_HEAD_DIM ** -0.5), NEG_INF)
        m = jnp.maximum(jnp.max(s, axis=-1, keepdims=True), sink)
        p = jnp.exp(s - m)
        inv = 1.0 / (jnp.sum(p, axis=-1, keepdims=True) + jnp.exp(sink - m))
        p = p.astype(BF16)
        for h in range(A_KV_HEADS):
            vv = jnp.concatenate([cv[h], vnt[h]], axis=1).astype(BF16)
            hr = slice(h * hrows, (h + 1) * hrows)
            o4 = lax.dot_general(p[hr], vv, (((1,), (1,)), ((), ())), preferred_element_type=F32) * inv[hr]
            o_sc[r0:r0 + SAMPLE_T, h * A_KV_W:(h + 1) * A_KV_W] = jnp.concatenate(
                [o4[gg * SAMPLE_T:(gg + 1) * SAMPLE_T] for gg in range(A_GROUP)], axis=1)
    o_ref[...] = o_sc[...].astype(o_ref.dtype)


def _swa_sample(proj, sink, l, cache_k, cache_v, prev_k, prev_v, *, name):
    depth, nseq = cache_k.shape[:2]
    g = SWA_SAMPLE_G
    rows = g * SAMPLE_T
    ka, va = COL_KA // A_KV_W, COL_VA // A_KV_W
    cache_spec = pl.BlockSpec((None, g, A_KV_HEADS, A_HEAD_DIM, WINDOW), lambda i: (l, i, 0, 0, 0))
    cache_shape = jax.ShapeDtypeStruct(cache_k.shape, F32)
    in_specs = [_smem_spec(),
                pl.BlockSpec((rows, BRANCH_W), lambda i: (i, COL_QA // BRANCH_W)),
                pl.BlockSpec((rows, A_KV_W), lambda i: (i, ka)),
                pl.BlockSpec((rows, A_KV_W), lambda i: (i, va)),
                cache_spec, cache_spec]
    args = [sink, proj, proj, proj, cache_k, cache_v]
    aliases = {}
    if prev_k is not None:
        in_specs += [pl.BlockSpec(memory_space=pl.ANY)] * 2
        aliases = {len(args): 1, len(args) + 1: 2}
        args += [prev_k, prev_v]
    return pl.pallas_call(
        functools.partial(_swa_sample_body, layer=l, aliased=prev_k is not None),
        out_shape=(jax.ShapeDtypeStruct((nseq * SAMPLE_T, BRANCH_W), BF16), cache_shape, cache_shape),
        grid=(nseq // g,),
        in_specs=in_specs,
        out_specs=(pl.BlockSpec((rows, BRANCH_W), lambda i: (i, 0)), cache_spec, cache_spec),
        scratch_shapes=[pltpu.VMEM((rows, BRANCH_W), F32)],
        input_output_aliases=aliases,
        compiler_params=_params(("parallel",)),
        name=name,
    )(*args)


def _lru_gates(xc, wa_ref, ba_ref, wx_ref, bx_ref, lam_ref):
    xb = xc.astype(BF16)
    ngroups, gw, _ = wa_ref.shape

    def blockdiag(w_ref):
        return jnp.concatenate(
            [jnp.dot(xb[:, c * gw:(c + 1) * gw], w_ref[c], preferred_element_type=F32)
             for c in range(ngroups)], axis=1)

    r = jax.nn.sigmoid(blockdiag(wa_ref) + ba_ref[...])
    i = jax.nn.sigmoid(blockdiag(wx_ref) + bx_ref[...])
    nl = -lam_ref[...]
    softplus = jnp.maximum(nl, 0.0) + jnp.log1p(jnp.exp(-jnp.abs(nl)))
    log_a = (-LRU_C * r) * softplus
    a = jnp.exp(log_a)
    th = jnp.tanh(log_a)
    mult = jnp.sqrt((-2.0 * th) / (1.0 - th))
    return a, mult, i


LRU_ROWS = 1024


def _lru_prompt_body(x_ref, y_ref, cw_ref, cb_ref, wa_ref, ba_ref, wx_ref, bx_ref, lam_ref,
                     o_ref, hlast_ref, xs_ref, a_ref, u_ref, h_ref):
    t = pl.program_id(1)
    rows, width = x_ref.shape

    @pl.when(t == 0)
    def _():
        xs_ref[0:SUBLANES, :] = jnp.zeros((SUBLANES, width), F32)
        h_ref[...] = jnp.zeros((SUBLANES, width), F32)

    x = x_ref[...].astype(F32)
    xs_ref[SUBLANES:SUBLANES + rows, :] = x
    cw = cw_ref[...]
    taps = [xs_ref[SUBLANES - 3 + j:SUBLANES - 3 + j + rows, :] * cw[j:j + 1] for j in range(CONV_WIDTH - 1)]
    taps.append(x * cw[CONV_WIDTH - 1:CONV_WIDTH])
    xc = cb_ref[...] + (((taps[0] + taps[1]) + taps[2]) + taps[3])
    xs_ref[0:SUBLANES, :] = xs_ref[rows:rows + SUBLANES, :]

    a, mult, i = _lru_gates(xc, wa_ref, ba_ref, wx_ref, bx_ref, lam_ref)
    first = (lax.broadcasted_iota(jnp.int32, (rows, 1), 0) == 0) & (t == 0)
    mult = jnp.where(first, 1.0, mult)
    a_ref[...] = a
    u_ref[...] = mult * (i * xc)

    rid = lax.broadcasted_iota(jnp.int32, (SUBLANES, width), 0)

    def tile(j, h):
        off = pl.multiple_of(j * SUBLANES, SUBLANES)
        at = a_ref[pl.ds(off, SUBLANES), :]
        ut = u_ref[pl.ds(off, SUBLANES), :]
        for d in (1, 2, 4):
            keep = rid >= d
            a_sh = jnp.where(keep, pltpu.roll(at, d, 0), 1.0)
            u_sh = jnp.where(keep, pltpu.roll(ut, d, 0), 0.0)
            ut = at * u_sh + ut
            at = at * a_sh
        hh = at * h + ut
        u_ref[pl.ds(off, SUBLANES), :] = hh
        return jnp.broadcast_to(hh[SUBLANES - 1:SUBLANES, :], (SUBLANES, width))

    h = lax.fori_loop(0, rows // SUBLANES, tile, h_ref[...], unroll=2)
    h_ref[...] = h
    hlast_ref[0] = h[0:1, :]
    o_ref[...] = (jax.nn.gelu(y_ref[...].astype(F32)) * u_ref[...]).astype(o_ref.dtype)


def _lru_prompt(proj, l, batch, seq, conv_w, conv_b, wa, ba, wx, bx, lam, *, name):
    rows = min(LRU_ROWS, seq)
    nt = seq // rows
    w = BRANCH_W
    vec = pl.BlockSpec((None, 1, w), lambda b, t: (l, 0, 0))
    gate_w = pl.BlockSpec((None,) + wa.shape[1:], lambda b, t: (l, 0, 0, 0))
    return pl.pallas_call(
        _lru_prompt_body,
        out_shape=(jax.ShapeDtypeStruct((batch * seq, w), BF16), jax.ShapeDtypeStruct((batch, 1, w), F32)),
        grid=(batch, nt),
        in_specs=[pl.BlockSpec((rows, w), lambda b, t: (b * nt + t, COL_XR // w)),
                  pl.BlockSpec((rows, w), lambda b, t: (b * nt + t, COL_YR // w)),
                  pl.BlockSpec((None, CONV_WIDTH, w), lambda b, t: (l, 0, 0)),
                  vec, gate_w, vec, gate_w, vec, vec],
        out_specs=(pl.BlockSpec((rows, w), lambda b, t: (b * nt + t, 0)),
                   pl.BlockSpec((1, 1, w), lambda b, t: (b, 0, 0))),
        scratch_shapes=[pltpu.VMEM((rows + SUBLANES, w), F32), pltpu.VMEM((rows, w), F32),
                        pltpu.VMEM((rows, w), F32), pltpu.VMEM((SUBLANES, w), F32)],
        compiler_params=_params(("parallel", "arbitrary")),
        name=name,
    )(proj, proj, conv_w, conv_b, wa, ba, wx, bx, lam)


def _lru_sample_body(x_ref, y_ref, c_ref, h0_ref, cw_ref, cb_ref, wa_ref, ba_ref,
                     wx_ref, bx_ref, lam_ref, o_ref, co_ref, ho_ref, xs_ref, ys_ref, os_ref):
    nseq = h0_ref.shape[0]
    xs_ref[...] = x_ref[...].astype(F32)
    ys_ref[...] = y_ref[...].astype(F32)

    def step(ref, t):
        return ref[pl.ds(t, nseq, stride=SAMPLE_T), :]

    hist = [c_ref[j] for j in range(CONV_WIDTH - 1)] + [step(xs_ref, t) for t in range(SAMPLE_T)]
    cw = cw_ref[...]
    xcs = []
    for t in range(SAMPLE_T):
        taps = [hist[t + j] * cw[j:j + 1] for j in range(CONV_WIDTH)]
        xcs.append(cb_ref[...] + (((taps[0] + taps[1]) + taps[2]) + taps[3]))
    xc = jnp.concatenate(xcs, axis=0)
    a, mult, i = _lru_gates(xc, wa_ref, ba_ref, wx_ref, bx_ref, lam_ref)
    u = mult * (i * xc)
    h = h0_ref[...]
    for t in range(SAMPLE_T):
        h = a[t * nseq:(t + 1) * nseq] * h + u[t * nseq:(t + 1) * nseq]
        os_ref[pl.ds(t, nseq, stride=SAMPLE_T), :] = jax.nn.gelu(step(ys_ref, t)) * h
    o_ref[...] = os_ref[...].astype(o_ref.dtype)
    ho_ref[...] = h
    for j in range(CONV_WIDTH - 1):
        co_ref[j] = hist[SAMPLE_T + j]


def _lru_sample(proj, l, state_conv_t, state_lru, conv_w, conv_b, wa, ba, wx, bx, lam, *, name):
    _, nseq, w = state_lru.shape
    rows = nseq * SAMPLE_T
    cw = LANES
    nc = w // cw
    assert wa.shape[1:] == (nc, cw, cw)
    vec = pl.BlockSpec((None, 1, cw), lambda c: (l, 0, c))
    gate_w = pl.BlockSpec((None, 1, cw, cw), lambda c: (l, c, 0, 0))
    ob, conv_new, h = pl.pallas_call(
        _lru_sample_body,
        out_shape=(jax.ShapeDtypeStruct((rows, w), BF16),
                   jax.ShapeDtypeStruct((CONV_WIDTH - 1, nseq, w), F32),
                   jax.ShapeDtypeStruct((nseq, w), F32)),
        grid=(nc,),
        in_specs=[pl.BlockSpec((rows, cw), lambda c: (0, COL_XR // cw + c)),
                  pl.BlockSpec((rows, cw), lambda c: (0, COL_YR // cw + c)),
                  pl.BlockSpec((None, CONV_WIDTH - 1, nseq, cw), lambda c: (l, 0, 0, c)),
                  pl.BlockSpec((None, nseq, cw), lambda c: (l, 0, c)),
                  pl.BlockSpec((None, CONV_WIDTH, cw), lambda c: (l, 0, c)),
                  vec, gate_w, vec, gate_w, vec, vec],
        out_specs=(pl.BlockSpec((rows, cw), lambda c: (0, c)),
                   pl.BlockSpec((CONV_WIDTH - 1, nseq, cw), lambda c: (0, 0, c)),
                   pl.BlockSpec((nseq, cw), lambda c: (0, c))),
        scratch_shapes=[pltpu.VMEM((rows, cw), F32), pltpu.VMEM((rows, cw), F32), pltpu.VMEM((rows, cw), F32)],
        compiler_params=_params(("parallel",)),
        name=name,
    )(proj, proj, state_conv_t, state_lru, conv_w, conv_b, wa, ba, wx, bx, lam)
    return ob, conv_new, h


def _rotate_pairs(x, cos, sin_signed, even_lane):
    width = x.shape[1]
    partner = jnp.where(even_lane, pltpu.roll(x, width - 1, 1), pltpu.roll(x, 1, 1))
    return x * cos + partner * sin_signed


def _retention_head(qh, kh, vh, s_prev, dmat, qdec, kdec, cdec):
    inner = lax.dot_general(qh.astype(BF16), kh.astype(BF16), (((1,), (1,)), ((), ())),
                            preferred_element_type=F32) * dmat
    o = jnp.dot(inner.astype(BF16), vh, preferred_element_type=F32)
    o = o + jnp.dot((qh * qdec).astype(BF16), s_prev.astype(BF16), preferred_element_type=F32)
    s_new = cdec * s_prev + lax.dot_general((kh * kdec).astype(BF16), vh, (((0,), (0,)), ((), ())),
                                            preferred_element_type=F32)
    return o, s_new


def _group_norm_gate(o, gate, gain):
    mu = jnp.mean(o, axis=-1, keepdims=True)
    oc = o - mu
    var = jnp.mean(oc * oc, axis=-1, keepdims=True)
    return jax.nn.silu(gate) * (oc * lax.rsqrt(var + EPS) * gain)


def _ret_prompt_body(cdec_ref, q_ref, k_ref, v_ref, g_ref, cos_ref, sin_ref, dmat_ref, qdec_ref, kdec_ref,
                     gn_ref, o_ref, s_ref):
    @pl.when(pl.program_id(1) == 0)
    def _():
        s_ref[...] = jnp.zeros(s_ref.shape, F32)

    rows, width = q_ref.shape
    even = (lax.broadcasted_iota(jnp.int32, (rows, width), 1) & 1) == 0
    cos = jnp.concatenate([cos_ref[...]] * R_HEADS, axis=1)
    sin = jnp.concatenate([sin_ref[...]] * R_HEADS, axis=1)
    qr = _rotate_pairs(q_ref[...].astype(F32), cos, sin, even)
    kr = _rotate_pairs(k_ref[...].astype(F32), cos, sin, even) * (R_HEAD_DIM ** -0.5)
    v = v_ref[...]
    gate = g_ref[...].astype(F32)
    qdec = qdec_ref[...]
    kdec = kdec_ref[...]
    gn = gn_ref[...]
    for c in range(rows // RET_CHUNK):
        rs = slice(c * RET_CHUNK, (c + 1) * RET_CHUNK)
        for h in range(R_HEADS):
            sl = slice(h * R_HEAD_DIM, (h + 1) * R_HEAD_DIM)
            o, s_new = _retention_head(qr[rs, sl], kr[rs, sl], v[rs, sl], s_ref[0, h], dmat_ref[h],
                                       qdec[:, sl], kdec[:, sl], cdec_ref[h])
            s_ref[0, h] = s_new
            o_ref[rs, sl] = _group_norm_gate(o, gate[rs, sl], gn[:, sl]).astype(o_ref.dtype)


def _ret_tables(pos, chunk):
    half = R_HEAD_DIM // 2
    inv = 1.0 / (RET_THETA ** jnp.linspace(0.0, 1.0, half, dtype=F32))
    ang = pos.astype(F32)[:, None] * jnp.repeat(inv, 2)[None, :]
    cos = jnp.cos(ang)
    sin_signed = jnp.sin(ang) * jnp.tile(jnp.array([-1.0, 1.0], F32), half)[None, :]
    lg = jnp.log1p(-jnp.exp2(-5.0 - jnp.arange(R_HEADS, dtype=F32)))
    n = jnp.arange(chunk, dtype=F32)
    diff = n[:, None] - n[None, :]
    dmat = jnp.where(diff >= 0, jnp.exp(jnp.maximum(diff, 0.0)[None] * lg[:, None, None]), 0.0)
    qdec = jnp.repeat(jnp.exp((n[:, None] + 1.0) * lg[None, :]), R_HEAD_DIM, axis=1)
    kdec = jnp.repeat(jnp.exp((chunk - 1.0 - n)[:, None] * lg[None, :]), R_HEAD_DIM, axis=1)
    cdec = jnp.exp(chunk * lg)
    return cos, sin_signed, dmat, qdec, kdec, cdec


RET_STEP_CHUNKS = 4


def _ret_prompt(proj, l, batch, seq, ret_gn, tables, *, name):
    c = RET_CHUNK
    rows = min(seq, RET_STEP_CHUNKS * c)
    nc = seq // rows
    w = BRANCH_W
    cos, sin, dmat, qdec, kdec, cdec = tables

    def col(off):
        return pl.BlockSpec((rows, w), lambda b, n: (b * nc + n, off // w))

    const2 = lambda b, n: (0, 0)
    return pl.pallas_call(
        _ret_prompt_body,
        out_shape=(jax.ShapeDtypeStruct((batch * seq, w), BF16),
                   jax.ShapeDtypeStruct((batch, R_HEADS, R_HEAD_DIM, R_HEAD_DIM), F32)),
        grid=(batch, nc),
        in_specs=[_smem_spec(), col(COL_QC), col(COL_KC), col(COL_VC), col(COL_GC),
                  pl.BlockSpec((rows, R_HEAD_DIM), lambda b, n: (n, 0)),
                  pl.BlockSpec((rows, R_HEAD_DIM), lambda b, n: (n, 0)),
                  pl.BlockSpec((R_HEADS, c, c), lambda b, n: (0, 0, 0)),
                  pl.BlockSpec((c, w), const2), pl.BlockSpec((c, w), const2),
                  pl.BlockSpec((None, 1, w), lambda b, n: (l, 0, 0))],
        out_specs=(pl.BlockSpec((rows, w), lambda b, n: (b * nc + n, 0)),
                   pl.BlockSpec((1, R_HEADS, R_HEAD_DIM, R_HEAD_DIM), lambda b, n: (b, 0, 0, 0))),
        compiler_params=_params(("parallel", "arbitrary")),
        name=name,
    )(cdec, proj, proj, proj, proj, cos, sin, dmat, qdec, kdec, ret_gn)


RET_SAMPLE_G = 8


def _ret_sample_body(*refs):
    (cdec_ref, q_ref, k_ref, v_ref, g_ref, cos_ref, sin_ref, dmat_ref, qdec_ref, kdec_ref,
     gn_ref, s_ref) = refs[:12]
    o_ref, so_ref, o_sc = refs[-3:]
    rows, width = q_ref.shape
    even = (lax.broadcasted_iota(jnp.int32, (rows, width), 1) & 1) == 0
    cos = cos_ref[...]
    sin = sin_ref[...]
    qr = _rotate_pairs(q_ref[...].astype(F32), cos, sin, even)
    kr = _rotate_pairs(k_ref[...].astype(F32), cos, sin, even) * (R_HEAD_DIM ** -0.5)
    vf = v_ref[...].astype(F32)
    gate = g_ref[...].astype(F32)
    qdec = qdec_ref[...]
    kdec = kdec_ref[...]
    gn = gn_ref[...]
    for g in range(RET_SAMPLE_G):
        rs = slice(g * SAMPLE_T, (g + 1) * SAMPLE_T)
        for h in range(R_HEADS):
            sl = slice(h * R_HEAD_DIM, (h + 1) * R_HEAD_DIM)
            o, s_new = _retention_head(qr[rs, sl], kr[rs, sl], vf[rs, sl].astype(BF16), s_ref[g, h],
                                       dmat_ref[h], qdec[rs, sl], kdec[rs, sl], cdec_ref[h])
            so_ref[g, h] = s_new
            o_sc[rs, sl] = _group_norm_gate(o, gate[rs, sl], gn[:, sl])
    o_ref[...] = o_sc[...].astype(o_ref.dtype)


def _ret_sample(proj, l, state, prev, ret_gn, tables, *, name):
    nseq = state.shape[1]
    g = RET_SAMPLE_G
    rows = g * SAMPLE_T
    w = BRANCH_W
    cos, sin, dmat, qdec, kdec, cdec = tables

    def col(off):
        return pl.BlockSpec((rows, w), lambda i: (i, off // w))

    const2 = lambda i: (0, 0)
    tab = pl.BlockSpec((rows, w), const2)
    state_spec = pl.BlockSpec((None, g, R_HEADS, R_HEAD_DIM, R_HEAD_DIM), lambda i: (l, i, 0, 0, 0))
    in_specs = [_smem_spec(), col(COL_QC), col(COL_KC), col(COL_VC), col(COL_GC), tab, tab,
                pl.BlockSpec((R_HEADS, SAMPLE_T, SAMPLE_T), lambda i: (0, 0, 0)),
                tab, tab, pl.BlockSpec((None, 1, w), lambda i: (l, 0, 0)), state_spec]
    args = [cdec, proj, proj, proj, proj, cos, sin, dmat, qdec, kdec, ret_gn, state]
    aliases = {}
    if prev is not None:
        in_specs.append(pl.BlockSpec(memory_space=pl.ANY))
        aliases = {len(args): 1}
        args.append(prev)
    return pl.pallas_call(
        _ret_sample_body,
        out_shape=(jax.ShapeDtypeStruct((nseq * SAMPLE_T, w), BF16), jax.ShapeDtypeStruct(state.shape, F32)),
        grid=(nseq // g,),
        in_specs=in_specs,
        out_specs=(pl.BlockSpec((rows, w), lambda i: (i, 0)), state_spec),
        scratch_shapes=[pltpu.VMEM((rows, w), F32)],
        input_output_aliases=aliases,
        compiler_params=_params(("parallel",)),
        name=name,
    )(*args)


def _ret_sample_tables(g):
    pos = PAST_LEN + jnp.arange(SAMPLE_T)
    cos, sin, dmat, qdec, kdec, cdec = _ret_tables(pos, SAMPLE_T)
    tile_rows = lambda x: jnp.tile(x, (g, 1))
    return (tile_rows(jnp.tile(cos, (1, R_HEADS))), tile_rows(jnp.tile(sin, (1, R_HEADS))), dmat,
            tile_rows(qdec), tile_rows(kdec), cdec)


def _cast_once(first, pairs):
    @pl.when(first)
    def _():
        for src, dst in pairs:
            dst[...] = src[...].astype(BF16)


def _softmax_pv(s, v):
    p = jnp.exp(s - jnp.max(s, axis=-1, keepdims=True))
    denom = jnp.sum(p, axis=-1, keepdims=True)
    return jnp.dot(p.astype(BF16), v, preferred_element_type=F32) / denom


def _cross_prompt_body(h_ref, g_ref, wq_ref, wo_ref, k_ref, v_ref, o_ref, wqb_ref, wob_ref):
    _cast_once((pl.program_id(0) == 0) & (pl.program_id(1) == 0), ((wq_ref, wqb_ref), (wo_ref, wob_ref)))
    h = h_ref[...]
    u = _rms(h, g_ref[...]).astype(BF16)
    q = jnp.dot(u, wqb_ref[...], preferred_element_type=F32).astype(BF16)
    heads = []
    for hh in range(X_HEADS):
        sl = slice(hh * X_HEAD_DIM, (hh + 1) * X_HEAD_DIM)
        k = k_ref[0, :, sl].astype(BF16)
        s = lax.dot_general(q[:, sl], k, (((1,), (1,)), ((), ())), preferred_element_type=F32) * (X_HEAD_DIM ** -0.5)
        heads.append(_softmax_pv(s, v_ref[0, :, sl].astype(BF16)).astype(BF16))
    o = jnp.concatenate(heads, axis=1)
    o_ref[...] = h + jnp.dot(o, wob_ref[...], preferred_element_type=F32)


def _cross_prompt(h, g, w_xq, w_xo, l, mem_k, mem_v, batch, seq, *, name):
    d = h.shape[1]
    tq = min(seq, 512)
    nq = seq // tq
    mem = mem_k.shape[1]
    kv = pl.BlockSpec((1, mem, X_W), lambda b, i: (b, 0, 0))
    row = pl.BlockSpec((tq, d), lambda b, i: (b * nq + i, 0))
    return pl.pallas_call(
        _cross_prompt_body,
        out_shape=jax.ShapeDtypeStruct(h.shape, F32),
        grid=(batch, nq),
        in_specs=[row,
                  pl.BlockSpec((None, 1, d), lambda b, i: (l, 0, 0)),
                  pl.BlockSpec((None, d, X_W), lambda b, i: (l, 0, 0)),
                  pl.BlockSpec((None, X_W, d), lambda b, i: (l, 0, 0)),
                  kv, kv],
        out_specs=row,
        scratch_shapes=[pltpu.VMEM((d, X_W), BF16), pltpu.VMEM((X_W, d), BF16)],
        compiler_params=_params(("arbitrary", "arbitrary")),
        name=name,
    )(h, g, w_xq, w_xo, mem_k, mem_v)


CROSS_SAMPLE_G = 8


def _cross_sample_body(h_ref, g_ref, wq_ref, wo_ref, k_ref, v_ref, o_ref, wqb_ref, wob_ref, o_sc):
    _cast_once(pl.program_id(0) == 0, ((wq_ref, wqb_ref), (wo_ref, wob_ref)))
    h = h_ref[...]
    u = _rms(h, g_ref[...]).astype(BF16)
    qf = jnp.dot(u, wqb_ref[...], preferred_element_type=F32)
    grows = X_HEADS * SAMPLE_T
    rows = CROSS_SAMPLE_G * grows
    ncol = k_ref.shape[1]
    row_head = (lax.broadcasted_iota(jnp.int32, (rows, ncol), 0) // SAMPLE_T) % X_HEADS
    col_head = lax.broadcasted_iota(jnp.int32, (rows, ncol), 1) % X_HEADS
    mask = row_head == col_head
    scores = []
    for g in range(CROSS_SAMPLE_G):
        q_g = qf[g * SAMPLE_T:(g + 1) * SAMPLE_T]
        q4 = jnp.concatenate([q_g[:, hh * X_HEAD_DIM:(hh + 1) * X_HEAD_DIM] for hh in range(X_HEADS)],
                             axis=0).astype(BF16)
        scores.append(lax.dot_general(q4, k_ref[g].astype(BF16), (((1,), (1,)), ((), ())),
                                      preferred_element_type=F32))
    s = jnp.where(mask, jnp.concatenate(scores, axis=0) * (X_HEAD_DIM ** -0.5), NEG_INF)
    p = jnp.exp(s - jnp.max(s, axis=-1, keepdims=True))
    inv = 1.0 / jnp.sum(p, axis=-1, keepdims=True)
    p = p.astype(BF16)
    for g in range(CROSS_SAMPLE_G):
        gr = slice(g * grows, (g + 1) * grows)
        o4 = jnp.dot(p[gr], v_ref[g].astype(BF16), preferred_element_type=F32) * inv[gr]
        o_sc[g * SAMPLE_T:(g + 1) * SAMPLE_T, :] = jnp.concatenate(
            [o4[hh * SAMPLE_T:(hh + 1) * SAMPLE_T] for hh in range(X_HEADS)], axis=1)
    o_ref[...] = h + jnp.dot(o_sc[...].astype(BF16), wob_ref[...], preferred_element_type=F32)


def _cross_sample(h, g, w_xq, w_xo, l, mem_k, mem_v, *, name):
    _, nseq, mh, hd = mem_k.shape
    d = h.shape[1]
    gs = CROSS_SAMPLE_G
    rows = gs * SAMPLE_T
    kv = pl.BlockSpec((None, gs, mh, hd), lambda i: (l, i, 0, 0))
    row = pl.BlockSpec((rows, d), lambda i: (i, 0))
    return pl.pallas_call(
        _cross_sample_body,
        out_shape=jax.ShapeDtypeStruct(h.shape, F32),
        grid=(nseq // gs,),
        in_specs=[row,
                  pl.BlockSpec((None, 1, d), lambda i: (l, 0, 0)),
                  pl.BlockSpec((None, d, X_W), lambda i: (l, 0, 0)),
                  pl.BlockSpec((None, X_W, d), lambda i: (l, 0, 0)),
                  kv, kv],
        out_specs=row,
        scratch_shapes=[pltpu.VMEM((d, X_W), BF16), pltpu.VMEM((X_W, d), BF16), pltpu.VMEM((rows, X_W), F32)],
        compiler_params=_params(("arbitrary",)),
        name=name,
    )(h, g, w_xq, w_xo, mem_k, mem_v)


def _block_diag_groups(w, group_w):
    depth, nb, bs, _ = w.shape
    per = group_w // bs
    w = w.reshape(depth, nb // per, per, bs, bs)
    eye = jnp.eye(per, dtype=w.dtype)
    return jnp.einsum('lcipq,ij->lcipjq', w, eye).reshape(depth, nb // per, group_w, group_w).astype(BF16)


def kernel(x_prompt, x_sample, mem_prompt, cache_win_k, cache_win_v, state_conv, state_lru, state_ret, cache_mem_k, cache_mem_v, norm_mix, w_in, attn_sink, conv_w, conv_b, lru_wa, lru_ba, lru_wx, lru_bx, lru_lambda, ret_gn, w_branch, w_out, norm_cross, w_xq, w_xk, w_xv, w_xo, norm_ffn, w_up, w_down, norm_final):
    bp, tp, d = x_prompt.shape
    bs, ts, _ = x_sample.shape
    depth = w_in.shape[0]
    mlen = mem_prompt.shape[1]
    wb = cache_win_k.shape[2]
    assert ts == SAMPLE_T and wb == WINDOW and w_in.shape[2] == IN_W
    assert tp % WINDOW == 0 and tp % RET_CHUNK == 0

    w_branch2 = w_branch.reshape(depth, -1, w_branch.shape[-1])
    wa_p, wx_p = _block_diag_groups(lru_wa, LRU_GROUP_W), _block_diag_groups(lru_wx, LRU_GROUP_W)
    wa_s, wx_s = _block_diag_groups(lru_wa, LANES), _block_diag_groups(lru_wx, LANES)
    norm_mix3, norm_cross3, norm_ffn3 = map(_layer_vec, (norm_mix, norm_cross, norm_ffn))
    norm_final3 = norm_final.reshape(1, 1, d).astype(F32)
    conv_b3, ba3, bx3, lam3 = map(_layer_vec, (conv_b, lru_ba, lru_bx, lru_lambda))
    ret_gn3 = ret_gn.reshape(depth, 1, BRANCH_W).astype(F32)
    sink = attn_sink.astype(F32)
    win_k_t = cache_win_k.transpose(0, 1, 3, 4, 2)
    win_v_t = cache_win_v.transpose(0, 1, 3, 4, 2)
    conv_t = state_conv.transpose(0, 2, 1, 3)
    mem_k4 = cache_mem_k.reshape(depth, bs, mlen * X_HEADS, X_HEAD_DIM)
    mem_v4 = cache_mem_v.reshape(depth, bs, mlen * X_HEADS, X_HEAD_DIM)
    tables_p = _ret_tables(jnp.arange(tp), RET_CHUNK)
    tables_s = _ret_sample_tables(RET_SAMPLE_G)

    hp = x_prompt.reshape(bp * tp, d)
    hs = x_sample.reshape(bs * ts, d)
    mem2d = mem_prompt.reshape(bp * mlen, d)
    up = _rms_cast(hp, norm_mix3, 0, out_dtype=BF16, name='norm_mix_p')
    us = _rms_cast(hs, norm_mix3, 0, out_dtype=BF16, name='norm_mix_s')
    outs = {k: [] for k in ('p_wk', 'p_wv', 'p_conv', 'p_lru', 'p_ret', 'p_mk', 'p_mv', 's_conv', 's_lru')}
    s_wk = s_wv = s_ret = None
    y_prompt = y_sample = None
    for l in range(depth):
        last = l == depth - 1
        lru_p = (conv_w, conv_b3, wa_p, ba3, wx_p, bx3, lam3)
        lru_s = (conv_w, conv_b3, wa_s, ba3, wx_s, bx3, lam3)
        next_norm = (norm_final3, 0) if last else (norm_mix3, l + 1)

        mk = _ws_matmul(mem2d, w_xk, l, tn=X_W, out_dtype=F32, name=f'mem_k_{l}')
        mv = _ws_matmul(mem2d, w_xv, l, tn=X_W, out_dtype=F32, name=f'mem_v_{l}')
        proj, w_up_b, w_down_b, w_branch_b, w_out_b = _in_proj(
            up, w_in, l, convert=(w_up, w_down, w_branch2, w_out), name=f'in_proj_p{l}')
        w_branch_b = w_branch_b.reshape(w_branch.shape[1:])
        oa = _swa_prompt(proj, sink, l, bp, tp, name=f'swa_p{l}')
        ob, lru_last = _lru_prompt(proj, l, bp, tp, *lru_p, name=f'lru_p{l}')
        oc, ret_last = _ret_prompt(proj, l, bp, tp, ret_gn3, tables_p, name=f'ret_p{l}')
        hp = _merge_out(oa, ob, oc, w_branch_b, w_out_b, proj, hp, name=f'merge_out_p{l}')
        hp = _cross_prompt(hp, norm_cross3, w_xq, w_xo, l, mk.reshape(bp, mlen, X_W), mv.reshape(bp, mlen, X_W),
                           bp, tp, name=f'cross_p{l}')
        res = _ffn(hp, norm_ffn3, l, w_up_b, w_down_b, *next_norm, final=last, name=f'ffn_p{l}')
        if last:
            y_prompt = res
        else:
            hp, up = res

        proj_s = _in_proj(us, w_in, l, name=f'in_proj_s{l}')
        oa, s_wk, s_wv = _swa_sample(proj_s, sink, l, win_k_t, win_v_t, s_wk, s_wv, name=f'swa_s{l}')
        ob, conv_new, lru_new = _lru_sample(proj_s, l, conv_t, state_lru, *lru_s, name=f'lru_s{l}')
        oc, s_ret = _ret_sample(proj_s, l, state_ret, s_ret, ret_gn3, tables_s, name=f'ret_s{l}')
        hs = _merge_out(oa, ob, oc, w_branch_b, w_out_b, proj_s, hs, name=f'merge_out_s{l}')
        hs = _cross_sample(hs, norm_cross3, w_xq, w_xo, l, mem_k4, mem_v4, name=f'cross_s{l}')
        res = _ffn(hs, norm_ffn3, l, w_up_b, w_down_b, *next_norm, final=last, name=f'ffn_s{l}')
        if last:
            y_sample = res
        else:
            hs, us = res
        outs['s_conv'].append(conv_new)
        outs['s_lru'].append(lru_new)

        proj3 = proj.reshape(bp, tp, IN_W)
        outs['p_wk'].append(proj3[:, tp - wb:, COL_KA:COL_KA + A_KV_W].astype(F32).reshape(bp, wb, A_KV_HEADS, A_HEAD_DIM))
        outs['p_wv'].append(proj3[:, tp - wb:, COL_VA:COL_VA + A_KV_W].astype(F32).reshape(bp, wb, A_KV_HEADS, A_HEAD_DIM))
        outs['p_conv'].append(proj3[:, tp - (CONV_WIDTH - 1):, COL_XR:COL_XR + BRANCH_W].astype(F32))
        outs['p_lru'].append(lru_last.reshape(bp, BRANCH_W))
        outs['p_ret'].append(ret_last)
        outs['p_mk'].append(mk.reshape(bp, mlen, X_HEADS, X_HEAD_DIM))
        outs['p_mv'].append(mv.reshape(bp, mlen, X_HEADS, X_HEAD_DIM))

    st = lambda k: jnp.stack(outs[k])
    return (y_prompt.reshape(bp, tp, d), y_sample.reshape(bs, ts, d),
            st('p_wk'), st('p_wv'), st('p_conv'), st('p_lru'), st('p_ret'), st('p_mk'), st('p_mv'),
            s_wk.transpose(0, 1, 4, 2, 3), s_wv.transpose(0, 1, 4, 2, 3),
            st('s_conv').transpose(0, 2, 1, 3), st('s_lru'), s_ret)
```
